```python
import jax
import jax.numpy as jnp
from jax import lax
import numpy as np

D_MODEL = 1024
BATCH = 16
SEQ = 2048
DEPTH = 2

GRID_W = 64
CTX_LEN = 256

CONV_DIM = D_MODEL // 4
CONV_WIDTH = 31
NA_HEADS = 8
NA_HEAD_DIM = 64
NA_DIM = NA_HEADS * NA_HEAD_DIM
NA_ROWS_MAX = 8
NA_COLS = 16
LRU_DIM = D_MODEL // 4
LRU_BLOCKS = 4
LRU_BLOCK = LRU_DIM // LRU_BLOCKS
LRU_CONV = 4
LRU_C = 8.0
MIX_DIM = CONV_DIM + NA_DIM + LRU_DIM
D_FF = 4 * D_MODEL
EPS = 1e-6
NEG_INF = -1e30

SPLITS = [CONV_DIM, 2 * CONV_DIM, 2 * CONV_DIM + NA_DIM, 2 * CONV_DIM + 2 * NA_DIM,
          2 * CONV_DIM + 3 * NA_DIM, 2 * CONV_DIM + 3 * NA_DIM + LRU_DIM]
IN_DIM = 2 * CONV_DIM + 3 * NA_DIM + 2 * LRU_DIM
K_OFF = SPLITS[2]
LX_OFF = SPLITS[4]
LG_OFF = SPLITS[5]

kernel_name = "hybrid_conv_natten_rglru_dit"


def rmsnorm(t, g):
    tf = t.astype(jnp.float32)
    y = tf * lax.rsqrt(jnp.mean(tf * tf, axis=-1, keepdims=True) + EPS)
    return (y * g.astype(jnp.float32)).astype(t.dtype)


def layernorm(t, g, b):
    tf = t.astype(jnp.float32)
    mu = jnp.mean(tf, axis=-1, keepdims=True)
    var = jnp.mean(jnp.square(tf - mu), axis=-1, keepdims=True)
    y = (tf - mu) * lax.rsqrt(var + EPS) * g.astype(jnp.float32) + b.astype(jnp.float32)
    return y.astype(t.dtype)


def modulate(h, shift, scale):
    return h * (1 + scale) + shift


def depthwise_conv(u, w, b, pad_l, pad_r):
    out = lax.conv_general_dilated(
        u, w[:, None, :], window_strides=(1,), padding=[(pad_l, pad_r)],
        dimension_numbers=("NWC", "WIO", "NWC"), feature_group_count=u.shape[-1])
    return out + b


def conformer_conv(val, gate, w, b, ln_g, ln_b):
    u = val * jax.nn.sigmoid(gate)
    u = depthwise_conv(u, w, b, CONV_WIDTH // 2, CONV_WIDTH // 2)
    return jax.nn.silu(layernorm(u, ln_g, ln_b))


def to_heads(t):
    b, n, _ = t.shape
    return t.reshape(b, n, NA_HEADS, NA_HEAD_DIM).transpose(0, 2, 1, 3)


def na_context(q, k, v):
    b, n, _ = q.shape
    qh, kh, vh = to_heads(q), to_heads(k), to_heads(v)
    s = jnp.einsum("bhqd,bhkd->bhqk", qh, kh, preferred_element_type=jnp.float32) * NA_HEAD_DIM ** -0.5
    p = jax.nn.softmax(s, axis=-1).astype(vh.dtype)
    o = jnp.einsum("bhqk,bhkd->bhqd", p, vh)
    return o.transpose(0, 2, 1, 3).reshape(b, n, NA_DIM)


def na_latent(q, k, v, kc, vc, rpb):
    b, n, _ = q.shape
    rows = n // GRID_W
    kr = min(NA_ROWS_MAX, rows)
    grid = lambda t: t.reshape(b, rows, GRID_W, NA_HEADS, NA_HEAD_DIM).transpose(0, 3, 1, 2, 4)
    qg, kg, vg = grid(q), grid(k), grid(v)
    kch, vch = to_heads(kc), to_heads(vc)
    scale = NA_HEAD_DIM ** -0.5
    qcol = jnp.arange(GRID_W)
    col_start = jnp.clip(qcol - NA_COLS // 2, 0, GRID_W - NA_COLS)
    kcol = jnp.arange(GRID_W)
    valid = (kcol[None, :] >= col_start[:, None]) & (kcol[None, :] < col_start[:, None] + NA_COLS)
    cidx = jnp.clip(kcol[None, :] - qcol[:, None], -(NA_COLS - 1), NA_COLS - 1) + NA_COLS - 1

    def row_step(r):
        rs = jnp.clip(r - kr // 2, 0, rows - kr)
        ridx = rs + jnp.arange(kr) - r + NA_ROWS_MAX - 1
        bias = rpb[:, ridx[None, :, None], cidx[:, None, :]]
        bias = jnp.where(valid[:, None, :], bias, NEG_INF)
        qr = lax.dynamic_index_in_dim(qg, r, axis=2, keepdims=False)
        kb = lax.dynamic_slice_in_dim(kg, rs, kr, axis=2)
        vb = lax.dynamic_slice_in_dim(vg, rs, kr, axis=2)
        s_loc = jnp.einsum("bhqd,bhrkd->bhqrk", qr, kb, preferred_element_type=jnp.float32) * scale + bias
        s_loc = s_loc.reshape(b, NA_HEADS, GRID_W, kr * GRID_W)
        s_ctx = jnp.einsum("bhqd,bhcd->bhqc", qr, kch, preferred_element_type=jnp.float32) * scale
        p = jax.nn.softmax(jnp.concatenate([s_loc, s_ctx], axis=-1), axis=-1)
        p_loc = p[..., :kr * GRID_W].reshape(b, NA_HEADS, GRID_W, kr, GRID_W).astype(vb.dtype)
        p_ctx = p[..., kr * GRID_W:].astype(vch.dtype)
        return (jnp.einsum("bhqrk,bhrkd->bhqd", p_loc, vb)
                + jnp.einsum("bhqc,bhcd->bhqd", p_ctx, vch))

    out = lax.map(row_step, jnp.arange(rows))
    return out.transpose(1, 0, 3, 2, 4).reshape(b, n, NA_DIM)


def rglru_coeffs(u, wx, bx, wa, ba, lam):
    b, n, _ = u.shape
    ub = u.reshape(b, n, LRU_BLOCKS, LRU_BLOCK)
    gx = jax.nn.sigmoid(jnp.einsum("bnkd,kde->bnke", ub, wx).reshape(b, n, LRU_DIM) + bx)
    ga = jax.nn.sigmoid((jnp.einsum("bnkd,kde->bnke", ub, wa).reshape(b, n, LRU_DIM) + ba).astype(jnp.float32))
    log_a = -LRU_C * ga * jax.nn.softplus(-lam.astype(jnp.float32))
    a = jnp.exp(log_a)
    coef = jnp.sqrt(-jnp.expm1(2.0 * log_a))
    return a, coef * (gx * u).astype(jnp.float32)


def linear_scan(a, b, h0, reverse):
    def combine(e1, e2):
        a1, b1 = e1
        a2, b2 = e2
        return a1 * a2, a2 * b1 + b2
    a_cum, b_cum = lax.associative_scan(combine, (a, b), reverse=reverse, axis=1)
    return a_cum * h0[:, None, :] + b_cum


def rglru_direction(x_lat, x_ctx, conv_w, conv_b, wx, bx, wa, ba, lam, reverse):
    pad = (0, LRU_CONV - 1) if reverse else (LRU_CONV - 1, 0)
    a_c, b_c = rglru_coeffs(depthwise_conv(x_ctx, conv_w, conv_b, *pad), wx, bx, wa, ba, lam)
    h_c = linear_scan(a_c, b_c, jnp.zeros_like(b_c[:, 0]), reverse)
    h_last = h_c[:, 0] if reverse else h_c[:, -1]
    a_l, b_l = rglru_coeffs(depthwise_conv(x_lat, conv_w, conv_b, *pad), wx, bx, wa, ba, lam)
    h_l = linear_scan(a_l, b_l, h_last, reverse)
    return h_l, h_c


def sq_relu_mlp(h, w1, w2):
    return jnp.square(jax.nn.relu(h @ w1)) @ w2


def setup_inputs(seed: int = 0) -> dict:
    key = jax.random.key(seed)
    ks = jax.random.split(key, 32)
    f32 = jnp.float32

    def nrm(k, shape, scale):
        return jax.random.normal(k, shape, f32) * scale

    u = jax.random.uniform(ks[21], (DEPTH, 2, LRU_DIM), f32, 0.9, 0.999)
    a0 = u ** (1.0 / LRU_C)
    return {
        "x": nrm(ks[0], (BATCH, SEQ, D_MODEL), 1.0),
        "c": nrm(ks[1], (BATCH, D_MODEL), 1.0),
        "ctx": nrm(ks[2], (BATCH, CTX_LEN, D_MODEL), 1.0),
        "c_ctx": nrm(ks[3], (D_MODEL,), 1.0),
        "norm1_g": 1.0 + nrm(ks[4], (DEPTH, D_MODEL), 0.02),
        "norm2_g": 1.0 + nrm(ks[5], (DEPTH, D_MODEL), 0.02),
        "ada_w": nrm(ks[6], (DEPTH, D_MODEL, 6 * D_MODEL), 0.5 * D_MODEL ** -0.5),
        "ada_b": nrm(ks[7], (DEPTH, 6 * D_MODEL), 0.02),
        "w_in": nrm(ks[8], (DEPTH, D_MODEL, IN_DIM), D_MODEL ** -0.5),
        "w_out": nrm(ks[9], (DEPTH, MIX_DIM, D_MODEL), MIX_DIM ** -0.5),
        "conv_w": nrm(ks[10], (DEPTH, CONV_WIDTH, CONV_DIM), CONV_WIDTH ** -0.5),
        "conv_b": nrm(ks[11], (DEPTH, CONV_DIM), 0.02),
        "conv_ln_g": 1.0 + nrm(ks[12], (DEPTH, CONV_DIM), 0.02),
        "conv_ln_b": nrm(ks[13], (DEPTH, CONV_DIM), 0.02),
        "na_rpb": nrm(ks[14], (DEPTH, NA_HEADS, 2 * NA_ROWS_MAX - 1, 2 * NA_COLS - 1), 0.1),
        "lru_conv_w": nrm(ks[15], (DEPTH, 2, LRU_CONV, LRU_DIM), LRU_CONV ** -0.5),
        "lru_conv_b": nrm(ks[16], (DEPTH, 2, LRU_DIM), 0.02),
        "lru_wx": nrm(ks[17], (DEPTH, 2, LRU_BLOCKS, LRU_BLOCK, LRU_BLOCK), LRU_BLOCK ** -0.5),
        "lru_bx": nrm(ks[18], (DEPTH, 2, LRU_DIM), 0.02),
        "lru_wa": nrm(ks[19], (DEPTH, 2, LRU_BLOCKS, LRU_BLOCK, LRU_BLOCK), LRU_BLOCK ** -0.5),
        "lru_ba": nrm(ks[20], (DEPTH, 2, LRU_DIM), 0.02),
        "lru_lambda": jnp.log(a0) - jnp.log1p(-a0),
        "mlp_w1": nrm(ks[22], (DEPTH, D_MODEL, D_FF), D_MODEL ** -0.5),
        "mlp_w2": nrm(ks[23], (DEPTH, D_FF, D_MODEL), D_FF ** -0.5),
        "final_g": 1.0 + nrm(ks[24], (D_MODEL,), 0.02),
    }


def reference(x, c, ctx, c_ctx, norm1_g, norm2_g, ada_w, ada_b, w_in, w_out, conv_w, conv_b,
              conv_ln_g, conv_ln_b, na_rpb, lru_conv_w, lru_conv_b, lru_wx, lru_bx, lru_wa,
              lru_ba, lru_lambda, mlp_w1, mlp_w2, final_g):
    cx = ctx
    for l in range(DEPTH):
        last = l == DEPTH - 1
        mod = jax.nn.silu(c) @ ada_w[l] + ada_b[l]
        sh1, sc1, g1, sh2, sc2, g2 = jnp.split(mod[:, None, :], 6, axis=-1)
        n_cm = 2 if last else 6
        mod_c = jax.nn.silu(c_ctx) @ ada_w[l][:, :n_cm * D_MODEL] + ada_b[l][:n_cm * D_MODEL]
        mod_c = jnp.split(mod_c, n_cm)

        h = modulate(rmsnorm(x, norm1_g[l]), sh1, sc1)
        hc = modulate(rmsnorm(cx, norm1_g[l]), mod_c[0], mod_c[1])
        a_val, a_gate, q, k, v, r_x, r_gate = jnp.split(h @ w_in[l], SPLITS, axis=-1)
        if last:
            ck, cv = jnp.split(hc @ w_in[l][:, K_OFF:LX_OFF], 2, axis=-1)
            cr_x = hc @ w_in[l][:, LX_OFF:LG_OFF]
        else:
            ca_val, ca_gate, cq, ck, cv, cr_x, cr_gate = jnp.split(hc @ w_in[l], SPLITS, axis=-1)

        y_a = conformer_conv(a_val, a_gate, conv_w[l], conv_b[l], conv_ln_g[l], conv_ln_b[l])
        y_b = na_latent(q, k, v, ck, cv, na_rpb[l])
        h_f, hc_f = rglru_direction(r_x, cr_x, lru_conv_w[l, 0], lru_conv_b[l, 0], lru_wx[l, 0],
                                    lru_bx[l, 0], lru_wa[l, 0], lru_ba[l, 0], lru_lambda[l, 0], False)
        h_b, hc_b = rglru_direction(r_x, cr_x, lru_conv_w[l, 1], lru_conv_b[l, 1], lru_wx[l, 1],
                                    lru_bx[l, 1], lru_wa[l, 1], lru_ba[l, 1], lru_lambda[l, 1], True)
        y_c = jax.nn.gelu(r_gate) * (h_f + h_b).astype(r_gate.dtype)

        x = x + g1 * (jnp.concatenate([y_a, y_b, y_c], axis=-1) @ w_out[l])
        h2 = modulate(rmsnorm(x, norm2_g[l]), sh2, sc2)
        x = x + g2 * sq_relu_mlp(h2, mlp_w1[l], mlp_w2[l])

        if not last:
            yc_a = conformer_conv(ca_val, ca_gate, conv_w[l], conv_b[l], conv_ln_g[l], conv_ln_b[l])
            yc_b = na_context(cq, ck, cv)
            yc_c = jax.nn.gelu(cr_gate) * (hc_f + hc_b).astype(cr_gate.dtype)
            cx = cx + mod_c[2] * (jnp.concatenate([yc_a, yc_b, yc_c], axis=-1) @ w_out[l])
            h2c = modulate(rmsnorm(cx, norm2_g[l]), mod_c[3], mod_c[4])
            cx = cx + mod_c[5] * sq_relu_mlp(h2c, mlp_w1[l], mlp_w2[l])
    return rmsnorm(x, final_g)
```

```python
import functools

import jax
import jax.numpy as jnp
from jax import lax
from jax.experimental import pallas as pl
from jax.experimental.pallas import tpu as pltpu

GRID_W = 64
CONV_WIDTH = 31
NA_HEADS = 8
NA_HEAD_DIM = 64
NA_ROWS = 8
NA_COLS = 16
LRU_BLOCKS = 4
LRU_CONV = 4
LRU_C = 8.0
EPS = 1e-6
NEG_INF = -1e30

SUBLANES = 8
LANES = 128
HEADS_PER_STEP = 4
HEAD_LANES = HEADS_PER_STEP * NA_HEAD_DIM
VMEM_LIMIT_BYTES = 56 * 1024 * 1024

F32 = jnp.float32
BF16 = jnp.bfloat16


def _sigmoid(x):
    return 1.0 / (1.0 + jnp.exp(-x))


def _gelu_tanh(x):
    return 0.5 * x * (1.0 + jnp.tanh(0.7978845608028654 * (x + 0.044715 * (x * x * x))))


def _rms(x):
    return x * lax.rsqrt(jnp.mean(x * x, axis=-1, keepdims=True) + EPS)


def _params(*sem):
    return pltpu.CompilerParams(dimension_semantics=sem, vmem_limit_bytes=VMEM_LIMIT_BYTES)


def _mod_kernel(c_ref, w_ref, b_ref, o_ref):
    c = c_ref[...]
    s = c * _sigmoid(c)
    o_ref[0] = jnp.dot(s, w_ref[0], preferred_element_type=F32,
                       precision=lax.Precision.HIGHEST) + b_ref[0]


def _modulation(cc, ada_w, ada_b):
    depth, d, n = ada_w.shape
    r = cc.shape[0]
    tn = n // 4
    return pl.pallas_call(
        _mod_kernel,
        grid=(depth, n // tn),
        in_specs=[pl.BlockSpec((r, d), lambda l, j: (0, 0)),
                  pl.BlockSpec((1, d, tn), lambda l, j: (l, 0, j)),
                  pl.BlockSpec((1, 1, tn), lambda l, j: (l, 0, j))],
        out_specs=pl.BlockSpec((1, r, tn), lambda l, j: (l, 0, j)),
        out_shape=jax.ShapeDtypeStruct((depth, r, n), F32),
        compiler_params=_params("parallel", "parallel"),
        name="adaln_modulation",
    )(cc, ada_w, ada_b.reshape(depth, 1, n))


def _in_proj_kernel(x_ref, mod_ref, g_ref, w_ref, u_ref, q_ref, k_ref, v_ref, rx_ref, rg_ref, *,
                    conv_dim, na_dim, lru_dim):
    x = x_ref[0]
    h = _rms(x) * g_ref[...]
    h = h * (1.0 + mod_ref[0, 1:2, :]) + mod_ref[0, 0:1, :]
    hb = h.astype(BF16)

    def proj(off, width):
        return jnp.dot(hb, w_ref[:, off:off + width], preferred_element_type=F32)

    off = 0
    val = proj(off, conv_dim); off += conv_dim
    gate = proj(off, conv_dim); off += conv_dim
    u_ref[0] = val * _sigmoid(gate)
    q_ref[0] = (proj(off, na_dim) * (NA_HEAD_DIM ** -0.5)).astype(BF16); off += na_dim
    k_ref[0] = proj(off, na_dim).astype(BF16); off += na_dim
    v_ref[0] = proj(off, na_dim).astype(BF16); off += na_dim
    rx_ref[0] = proj(off, lru_dim); off += lru_dim
    rg_ref[0] = proj(off, lru_dim)


def _in_proj(x, mod, g, w_bf16, *, conv_dim, na_dim, lru_dim):
    b, n, d = x.shape
    tm = min(n, 512)
    per_batch = mod.shape[0] != 1
    tok = lambda width, dt: (pl.BlockSpec((1, tm, width), lambda i, j: (i, j, 0)),
                             jax.ShapeDtypeStruct((b, n, width), dt))
    outs = [tok(conv_dim, F32), tok(na_dim, BF16), tok(na_dim, BF16), tok(na_dim, BF16),
            tok(lru_dim, F32), tok(lru_dim, F32)]
    return pl.pallas_call(
        functools.partial(_in_proj_kernel, conv_dim=conv_dim, na_dim=na_dim, lru_dim=lru_dim),
        grid=(b, n // tm),
        in_specs=[pl.BlockSpec((1, tm, d), lambda i, j: (i, j, 0)),
                  pl.BlockSpec((1, 6, d), (lambda i, j: (i, 0, 0)) if per_batch else (lambda i, j: (0, 0, 0))),
                  pl.BlockSpec((1, d), lambda i, j: (0, 0)),
                  pl.BlockSpec(w_bf16.shape, lambda i, j: (0, 0))],
        out_specs=[o[0] for o in outs],
        out_shape=[o[1] for o in outs],
        compiler_params=_params("parallel", "parallel"),
        name="in_proj",
    )(x, mod, g, w_bf16)


CONV_TB = 32
CONV_PAD = 16


def _conformer_kernel(u_ref, w_ref, b_ref, lg_ref, lb_ref, o_ref, pad_ref, *, n):
    c = u_ref.shape[-1]
    zeros = jnp.zeros((CONV_PAD, c), F32)
    pad_ref[0:CONV_PAD, :] = zeros
    pad_ref[n + CONV_PAD:n + 2 * CONV_PAD, :] = zeros
    pad_ref[CONV_PAD:n + CONV_PAD, :] = u_ref[0]
    half = CONV_WIDTH // 2
    groups = (CONV_WIDTH + CONV_PAD - half + SUBLANES - 1) // SUBLANES

    def block(i, carry):
        start = pl.multiple_of(i * CONV_TB, CONV_TB)
        win = pad_ref[pl.ds(start, CONV_TB + 2 * CONV_PAD), :]
        acc = jnp.broadcast_to(b_ref[...], (CONV_TB, c))
        for r in range(SUBLANES):
            sh = win[r:r + CONV_TB + (groups - 1) * SUBLANES, :]
            for a in range(groups):
                k = r + SUBLANES * a - (CONV_PAD - half)
                if 0 <= k < CONV_WIDTH:
                    acc = acc + w_ref[k:k + 1, :] * sh[SUBLANES * a:SUBLANES * a + CONV_TB, :]
        mu = jnp.mean(acc, axis=-1, keepdims=True)
        cen = acc - mu
        var = jnp.mean(cen * cen, axis=-1, keepdims=True)
        y = cen * lax.rsqrt(var + EPS) * lg_ref[...] + lb_ref[...]
        o_ref[0, pl.ds(start, CONV_TB), :] = (y * _sigmoid(y)).astype(o_ref.dtype)
        return carry

    lax.fori_loop(0, n // CONV_TB, block, 0)


def _conformer(u, w, b, ln_g, ln_b):
    bsz, n, c = u.shape
    row = pl.BlockSpec((1, c), lambda i: (0, 0))
    return pl.pallas_call(
        functools.partial(_conformer_kernel, n=n),
        grid=(bsz,),
        in_specs=[pl.BlockSpec((1, n, c), lambda i: (i, 0, 0)),
                  pl.BlockSpec(w.shape, lambda i: (0, 0)), row, row, row],
        out_specs=pl.BlockSpec((1, n, c), lambda i: (i, 0, 0)),
        out_shape=jax.ShapeDtypeStruct((bsz, n, c), BF16),
        scratch_shapes=[pltpu.VMEM((n + 2 * CONV_PAD, c), F32)],
        compiler_params=_params("parallel"),
        name="conformer_conv",
    )(u, w, b.reshape(1, c), ln_g.reshape(1, c), ln_b.reshape(1, c))


def _head_of_lane():
    return lax.broadcasted_iota(jnp.int32, (1, HEAD_LANES), 1) // NA_HEAD_DIM


def _stack_heads(qr, head):
    zero = jnp.zeros_like(qr)
    return jnp.concatenate([jnp.where(head == h, qr, zero) for h in range(HEADS_PER_STEP)], axis=0)


def _unstack_heads(o, head, m):
    out = o[0:m]
    for h in range(1, HEADS_PER_STEP):
        out = jnp.where(head == h, o[h * m:(h + 1) * m], out)
    return out


def _na_latent_kernel(q_ref, k_ref, v_ref, kc_ref, vc_ref, bias_ref, o_ref, *, rows):
    head = _head_of_lane()
    kc = kc_ref[0]
    vc = vc_ref[0]
    band = NA_ROWS * GRID_W
    nt = (((1,), (1,)), ((), ()))

    def row_step(r, carry):
        rs = jnp.clip(r - NA_ROWS // 2, 0, rows - NA_ROWS)
        q0 = pl.multiple_of(r * GRID_W, GRID_W)
        k0 = pl.multiple_of(rs * GRID_W, GRID_W)
        qs = _stack_heads(q_ref[0, pl.ds(q0, GRID_W), :], head)
        kb = k_ref[0, pl.ds(k0, band), :]
        vb = v_ref[0, pl.ds(k0, band), :]
        s_loc = lax.dot_general(qs, kb, nt, preferred_element_type=F32) + bias_ref[0, r - rs]
        s_ctx = lax.dot_general(qs, kc, nt, preferred_element_type=F32)
        m = jnp.maximum(jnp.max(s_loc, axis=-1, keepdims=True), jnp.max(s_ctx, axis=-1, keepdims=True))
        p_loc = jnp.exp(s_loc - m)
        p_ctx = jnp.exp(s_ctx - m)
        denom = jnp.sum(p_loc, axis=-1, keepdims=True) + jnp.sum(p_ctx, axis=-1, keepdims=True)
        o = (jnp.dot(p_loc.astype(BF16), vb, preferred_element_type=F32)
             + jnp.dot(p_ctx.astype(BF16), vc, preferred_element_type=F32))
        o = o * (1.0 / denom)
        o_ref[0, pl.ds(q0, GRID_W), :] = _unstack_heads(o, head, GRID_W).astype(o_ref.dtype)
        return carry

    lax.fori_loop(0, rows, row_step, 0)


def _na_bias_table(rpb):
    h = rpb.shape[0]
    qcol = jnp.arange(GRID_W)
    kcol = jnp.arange(GRID_W)
    col_start = jnp.clip(qcol - NA_COLS // 2, 0, GRID_W - NA_COLS)
    valid = (kcol[None, :] >= col_start[:, None]) & (kcol[None, :] < col_start[:, None] + NA_COLS)
    cidx = jnp.clip(kcol[None, :] - qcol[:, None], -(NA_COLS - 1), NA_COLS - 1) + NA_COLS - 1
    ridx = jnp.arange(NA_ROWS)[None, :] - jnp.arange(NA_ROWS)[:, None] + NA_ROWS - 1
    t = rpb[:, ridx[:, :, None, None], cidx[None, None, :, :]]
    t = jnp.where(valid[None, None, None], t, NEG_INF)
    t = t.transpose(0, 1, 3, 2, 4).reshape(h // HEADS_PER_STEP, HEADS_PER_STEP, NA_ROWS, GRID_W,
                                           NA_ROWS * GRID_W)
    return t.transpose(0, 2, 1, 3, 4).reshape(h // HEADS_PER_STEP, NA_ROWS, HEADS_PER_STEP * GRID_W,
                                              NA_ROWS * GRID_W)


def _na_latent(q, k, v, kc, vc, bias):
    bsz, n, na_dim = q.shape
    nc = kc.shape[1]
    groups = na_dim // HEAD_LANES
    rows = n // GRID_W
    assert rows >= NA_ROWS
    lat = pl.BlockSpec((1, n, HEAD_LANES), lambda g, i: (i, 0, g))
    ctx = pl.BlockSpec((1, nc, HEAD_LANES), lambda g, i: (i, 0, g))
    return pl.pallas_call(
        functools.partial(_na_latent_kernel, rows=rows),
        grid=(groups, bsz),
        in_specs=[lat, lat, lat, ctx, ctx,
                  pl.BlockSpec((1,) + bias.shape[1:], lambda g, i: (g, 0, 0, 0))],
        out_specs=lat,
        out_shape=jax.ShapeDtypeStruct((bsz, n, na_dim), BF16),
        compiler_params=_params("parallel", "parallel"),
        name="na_latent",
    )(q, k, v, kc, vc, bias)


def _na_context_kernel(q_ref, k_ref, v_ref, o_ref):
    head = _head_of_lane()
    n = q_ref.shape[1]
    qs = _stack_heads(q_ref[0], head)
    s = lax.dot_general(qs, k_ref[0], (((1,), (1,)), ((), ())), preferred_element_type=F32)
    p = jnp.exp(s - jnp.max(s, axis=-1, keepdims=True))
    denom = jnp.sum(p, axis=-1, keepdims=True)
    o = jnp.dot(p.astype(BF16), v_ref[0], preferred_element_type=F32) * (1.0 / denom)
    o_ref[0] = _unstack_heads(o, head, n).astype(o_ref.dtype)


def _na_context(q, k, v):
    bsz, n, na_dim = q.shape
    blk = pl.BlockSpec((1, n, HEAD_LANES), lambda i, g: (i, 0, g))
    return pl.pallas_call(
        _na_context_kernel,
        grid=(bsz, na_dim // HEAD_LANES),
        in_specs=[blk, blk, blk],
        out_specs=blk,
        out_shape=jax.ShapeDtypeStruct((bsz, n, na_dim), BF16),
        compiler_params=_params("parallel", "parallel"),
        name="na_context",
    )(q, k, v)


LRU_TB = 256
LRU_PAD = 8


def _rglru_kernel(xl_ref, gl_ref, xc_ref, gc_ref, cw_ref, vec_ref, wx_ref, wa_ref, yl_ref, yc_ref,
                  padl_ref, padc_ref, a_ref, b_ref, h_ref, *, n_lat, n_ctx):
    c = xl_ref.shape[-1]
    zeros = jnp.zeros((LRU_PAD, c), F32)
    for pad_ref, src_ref, n in ((padc_ref, xc_ref, n_ctx), (padl_ref, xl_ref, n_lat)):
        pad_ref[0:LRU_PAD, :] = zeros
        pad_ref[n + LRU_PAD:n + 2 * LRU_PAD, :] = zeros
        pad_ref[LRU_PAD:n + LRU_PAD, :] = src_ref[0]
    sub = lax.broadcasted_iota(jnp.int32, (SUBLANES, c), 0)

    for direction in range(2):
        reverse = direction == 1
        conv_b = vec_ref[direction, 0:1, :]
        bx = vec_ref[direction, 1:2, :]
        ba = vec_ref[direction, 2:3, :]
        neg_lam = -vec_ref[direction, 3:4, :]
        softplus = jnp.maximum(neg_lam, 0.0) + jnp.log(1.0 + jnp.exp(-jnp.abs(neg_lam)))
        decay = -LRU_C * softplus

        def coeffs(pad_ref, base, n):
            tb = min(n, LRU_TB)

            def block(i, carry):
                start = pl.multiple_of(i * tb, tb)
                win = pad_ref[pl.ds(start, tb + 2 * LRU_PAD), :]
                u = jnp.broadcast_to(conv_b, (tb, c))
                for k in range(LRU_CONV):
                    off = LRU_PAD + k - (0 if reverse else LRU_CONV - 1)
                    u = u + cw_ref[direction, k:k + 1, :] * win[off:off + tb, :]
                ub = u.astype(BF16)
                gx = _sigmoid(jnp.dot(ub, wx_ref[direction], preferred_element_type=F32) + bx)
                ga = _sigmoid(jnp.dot(ub, wa_ref[direction], preferred_element_type=F32) + ba)
                log_a = decay * ga
                a_ref[pl.ds(base + start, tb), :] = jnp.exp(log_a)
                b_ref[pl.ds(base + start, tb), :] = jnp.sqrt(1.0 - jnp.exp(2.0 * log_a)) * (gx * u)
                return carry

            lax.fori_loop(0, n // tb, block, 0)

        coeffs(padc_ref, 0, n_ctx)
        coeffs(padl_ref, n_ctx, n_lat)

        def scan(base, n, hb):
            chunks = n // SUBLANES

            def chunk(j, hb):
                jj = chunks - 1 - j if reverse else j
                rows = pl.ds(pl.multiple_of(base + jj * SUBLANES, SUBLANES), SUBLANES)
                a = a_ref[rows, :]
                b = b_ref[rows, :]
                for d in (1, 2, 4):
                    shift = SUBLANES - d if reverse else d
                    keep = (sub < SUBLANES - d) if reverse else (sub >= d)
                    ra = pltpu.roll(a, shift, 0)
                    rb = pltpu.roll(b, shift, 0)
                    b = jnp.where(keep, a * rb + b, b)
                    a = jnp.where(keep, a * ra, a)
                h = a * hb + b
                if reverse:
                    h_ref[rows, :] = h_ref[rows, :] + h
                else:
                    h_ref[rows, :] = h
                last = 0 if reverse else SUBLANES - 1
                a_last = jnp.broadcast_to(a[last:last + 1, :], (SUBLANES, c))
                b_last = jnp.broadcast_to(b[last:last + 1, :], (SUBLANES, c))
                return a_last * hb + b_last

            return lax.fori_loop(0, chunks, chunk, hb)

        hb = scan(0, n_ctx, jnp.zeros((SUBLANES, c), F32))
        scan(n_ctx, n_lat, hb)

    def gate_out(g_ref, y_ref, base, n):
        tb = min(n, LRU_TB)

        def block(i, carry):
            start = pl.multiple_of(i * tb, tb)
            g = g_ref[0, pl.ds(start, tb), :]
            y_ref[0, pl.ds(start, tb), :] = (_gelu_tanh(g) * h_ref[pl.ds(base + start, tb), :]).astype(y_ref.dtype)
            return carry

        lax.fori_loop(0, n // tb, block, 0)

    gate_out(gc_ref, yc_ref, 0, n_ctx)
    gate_out(gl_ref, yl_ref, n_ctx, n_lat)


def _block_diag(w):
    two, nb, m, _ = w.shape
    eye = jnp.eye(nb, dtype=w.dtype)
    return (w[:, :, :, None, :] * eye[None, :, None, :, None]).reshape(two, nb * m, nb * m)


def _rglru(xl, gl, xc, gc, conv_w, conv_b, wx, bx, wa, ba, lam):
    bsz, n_lat, c = xl.shape
    n_ctx = xc.shape[1]
    vec = jnp.stack([conv_b, bx, ba, lam], axis=1)
    wxd = _block_diag(wx).astype(BF16)
    wad = _block_diag(wa).astype(BF16)
    lat = pl.BlockSpec((1, n_lat, c), lambda i: (i, 0, 0))
    ctx = pl.BlockSpec((1, n_ctx, c), lambda i: (i, 0, 0))
    full = lambda a: pl.BlockSpec(a.shape, lambda i: (0,) * a.ndim)
    return pl.pallas_call(
        functools.partial(_rglru_kernel, n_lat=n_lat, n_ctx=n_ctx),
        grid=(bsz,),
        in_specs=[lat, lat, ctx, ctx, full(conv_w), full(vec), full(wxd), full(wad)],
        out_specs=[lat, ctx],
        out_shape=[jax.ShapeDtypeStruct((bsz, n_lat, c), BF16), jax.ShapeDtypeStruct((bsz, n_ctx, c), BF16)],
        scratch_shapes=[pltpu.VMEM((n_lat + 2 * LRU_PAD, c), F32), pltpu.VMEM((n_ctx + 2 * LRU_PAD, c), F32),
                        pltpu.VMEM((n_ctx + n_lat, c), F32), pltpu.VMEM((n_ctx + n_lat, c), F32),
                        pltpu.VMEM((n_ctx + n_lat, c), F32)],
        compiler_params=_params("parallel"),
        name="rglru",
    )(xl, gl, xc, gc, conv_w, vec, wxd, wad)


FF_CHUNK = 1024


def _out_mlp_kernel(x_ref, ya_ref, yb_ref, yc_ref, mod_ref, g2_ref, gf_ref, wo_ref, w1_ref, w2_ref, o_ref, *,
                    final_norm):
    ca = ya_ref.shape[-1]
    cb = yb_ref.shape[-1]
    cc = yc_ref.shape[-1]
    mix = (jnp.dot(ya_ref[0], wo_ref[0:ca, :], preferred_element_type=F32)
           + jnp.dot(yb_ref[0], wo_ref[ca:ca + cb, :], preferred_element_type=F32)
           + jnp.dot(yc_ref[0], wo_ref[ca + cb:ca + cb + cc, :], preferred_element_type=F32))
    x1 = x_ref[0] + mod_ref[0, 2:3, :] * mix
    h2 = _rms(x1) * g2_ref[...]
    h2 = (h2 * (1.0 + mod_ref[0, 4:5, :]) + mod_ref[0, 3:4, :]).astype(BF16)
    acc = jnp.zeros(x1.shape, F32)
    for j in range(w1_ref.shape[1] // FF_CHUNK):
        cols = slice(j * FF_CHUNK, (j + 1) * FF_CHUNK)
        hid = jnp.maximum(jnp.dot(h2, w1_ref[:, cols], preferred_element_type=F32), 0.0)
        acc = acc + jnp.dot((hid * hid).astype(BF16), w2_ref[cols, :], preferred_element_type=F32)
    out = x1 + mod_ref[0, 5:6, :] * acc
    if final_norm:
        out = _rms(out) * gf_ref[...]
    o_ref[0] = out


def _out_mlp(x, ya, yb, yc, mod, g2, gf, wo, w1, w2, *, final_norm):
    b, n, d = x.shape
    tm = min(n, 512)
    per_batch = mod.shape[0] != 1
    tok = lambda a: pl.BlockSpec((1, tm, a.shape[-1]), lambda i, j: (i, j, 0))
    resident = lambda a: pl.BlockSpec(a.shape, lambda i, j: (0, 0), pipeline_mode=pl.Buffered(1))
    return pl.pallas_call(
        functools.partial(_out_mlp_kernel, final_norm=final_norm),
        grid=(b, n // tm),
        in_specs=[tok(x), tok(ya), tok(yb), tok(yc),
                  pl.BlockSpec((1, 6, d), (lambda i, j: (i, 0, 0)) if per_batch else (lambda i, j: (0, 0, 0))),
                  pl.BlockSpec((1, d), lambda i, j: (0, 0)), pl.BlockSpec((1, d), lambda i, j: (0, 0)),
                  resident(wo), resident(w1), resident(w2)],
        out_specs=tok(x),
        out_shape=jax.ShapeDtypeStruct((b, n, d), F32),
        compiler_params=_params("parallel", "parallel"),
        name="out_proj_mlp",
    )(x, ya, yb, yc, mod, g2, gf, wo, w1, w2)


def kernel(x, c, ctx, c_ctx, norm1_g, norm2_g, ada_w, ada_b, w_in, w_out, conv_w, conv_b, conv_ln_g, conv_ln_b,
           na_rpb, lru_conv_w, lru_conv_b, lru_wx, lru_bx, lru_wa, lru_ba, lru_lambda, mlp_w1, mlp_w2, final_g):
    depth = w_in.shape[0]
    bsz, _, d = x.shape
    conv_dim = conv_w.shape[-1]
    lru_dim = lru_conv_w.shape[-1]
    na_dim = NA_HEADS * NA_HEAD_DIM
    dims = dict(conv_dim=conv_dim, na_dim=na_dim, lru_dim=lru_dim)

    cond = jnp.concatenate([c, c_ctx[None, :]], axis=0)
    cond = jnp.pad(cond, ((0, -(bsz + 1) % SUBLANES), (0, 0)))
    mod_all = _modulation(cond, ada_w, ada_b)
    cx = ctx
    for l in range(depth):
        last = l == depth - 1
        mod = mod_all[l, :bsz].reshape(bsz, 6, d)
        mod_c = mod_all[l, bsz:bsz + 1].reshape(1, 6, d)
        w_in_l = w_in[l].astype(BF16)
        g1 = norm1_g[l].reshape(1, d)

        u, q, k, v, rx, rg = _in_proj(x, mod, g1, w_in_l, **dims)
        cu, cq, ck, cv, crx, crg = _in_proj(cx, mod_c, g1, w_in_l, **dims)

        y_a = _conformer(u, conv_w[l], conv_b[l], conv_ln_g[l], conv_ln_b[l])
        y_b = _na_latent(q, k, v, ck, cv, _na_bias_table(na_rpb[l]))
        y_c, yc_c = _rglru(rx, rg, crx, crg, lru_conv_w[l], lru_conv_b[l], lru_wx[l], lru_bx[l],
                           lru_wa[l], lru_ba[l], lru_lambda[l])

        w_out_l = w_out[l].astype(BF16)
        w1_l = mlp_w1[l].astype(BF16)
        w2_l = mlp_w2[l].astype(BF16)
        g2 = norm2_g[l].reshape(1, d)
        gf = final_g.reshape(1, d)
        x = _out_mlp(x, y_a, y_b, y_c, mod, g2, gf, w_out_l, w1_l, w2_l, final_norm=last)
        if not last:
            yc_a = _conformer(cu, conv_w[l], conv_b[l], conv_ln_g[l], conv_ln_b[l])
            yc_b = _na_context(cq, ck, cv)
            cx = _out_mlp(cx, yc_a, yc_b, yc_c, mod_c, g2, gf, w_out_l, w1_l, w2_l, final_norm=False)
    return x
```

```python
import functools

import jax
import jax.numpy as jnp
from jax import lax
from jax.experimental import pallas as pl
from jax.experimental.pallas import tpu as pltpu

GRID_W = 64
CONV_WIDTH = 31
NA_HEADS = 8
NA_HEAD_DIM = 64
NA_ROWS = 8
NA_COLS = 16
LRU_BLOCKS = 4
LRU_CONV = 4
LRU_C = 8.0
EPS = 1e-6
NEG_INF = -1e30

SUBLANES = 8
LANES = 128
HEADS_PER_STEP = 4
HEAD_LANES = HEADS_PER_STEP * NA_HEAD_DIM
VMEM_LIMIT_BYTES = 56 * 1024 * 1024

F32 = jnp.float32
BF16 = jnp.bfloat16


def _sigmoid(x):
    return 1.0 / (1.0 + jnp.exp(-x))


def _gelu_tanh(x):
    return 0.5 * x * (1.0 + jnp.tanh(0.7978845608028654 * (x + 0.044715 * (x * x * x))))


def _rms(x):
    return x * lax.rsqrt(jnp.mean(x * x, axis=-1, keepdims=True) + EPS)


def _params(*sem):
    return pltpu.CompilerParams(dimension_semantics=sem, vmem_limit_bytes=VMEM_LIMIT_BYTES)


def _mod_kernel(c_ref, w_ref, b_ref, o_ref):
    c = c_ref[...]
    s = c * _sigmoid(c)
    o_ref[0] = jnp.dot(s, w_ref[0], preferred_element_type=F32,
                       precision=lax.Precision.HIGHEST) + b_ref[0]


def _modulation(cc, ada_w, ada_b):
    depth, d, n = ada_w.shape
    r = cc.shape[0]
    tn = n // 4
    return pl.pallas_call(
        _mod_kernel,
        grid=(depth, n // tn),
        in_specs=[pl.BlockSpec((r, d), lambda l, j: (0, 0)),
                  pl.BlockSpec((1, d, tn), lambda l, j: (l, 0, j)),
                  pl.BlockSpec((1, 1, tn), lambda l, j: (l, 0, j))],
        out_specs=pl.BlockSpec((1, r, tn), lambda l, j: (l, 0, j)),
        out_shape=jax.ShapeDtypeStruct((depth, r, n), F32),
        compiler_params=_params("parallel", "parallel"),
        name="adaln_modulation",
    )(cc, ada_w, ada_b.reshape(depth, 1, n))


def _in_proj_kernel(x_ref, mod_ref, g_ref, w_ref, u_ref, q_ref, k_ref, v_ref, rx_ref, rg_ref, *,
                    conv_dim, na_dim, lru_dim):
    x = x_ref[0]
    h = _rms(x) * g_ref[...]
    h = h * (1.0 + mod_ref[0, 1:2, :]) + mod_ref[0, 0:1, :]
    hb = h.astype(BF16)

    def proj(off, width):
        return jnp.dot(hb, w_ref[:, off:off + width], preferred_element_type=F32)

    off = 0
    val = proj(off, conv_dim); off += conv_dim
    gate = proj(off, conv_dim); off += conv_dim
    u_ref[0] = val * _sigmoid(gate)
    q_ref[0] = (proj(off, na_dim) * (NA_HEAD_DIM ** -0.5)).astype(BF16); off += na_dim
    k_ref[0] = proj(off, na_dim).astype(BF16); off += na_dim
    v_ref[0] = proj(off, na_dim).astype(BF16); off += na_dim
    rx_ref[0] = proj(off, lru_dim); off += lru_dim
    rg_ref[0] = proj(off, lru_dim)


def _in_proj(x, mod, g, w_bf16, *, conv_dim, na_dim, lru_dim):
    b, n, d = x.shape
    tm = min(n, 512)
    per_batch = mod.shape[0] != 1
    tok = lambda width, dt: (pl.BlockSpec((1, tm, width), lambda i, j: (i, j, 0)),
                             jax.ShapeDtypeStruct((b, n, width), dt))
    outs = [tok(conv_dim, F32), tok(na_dim, BF16), tok(na_dim, BF16), tok(na_dim, BF16),
            tok(lru_dim, F32), tok(lru_dim, F32)]
    return pl.pallas_call(
        functools.partial(_in_proj_kernel, conv_dim=conv_dim, na_dim=na_dim, lru_dim=lru_dim),
        grid=(b, n // tm),
        in_specs=[pl.BlockSpec((1, tm, d), lambda i, j: (i, j, 0)),
                  pl.BlockSpec((1, 6, d), (lambda i, j: (i, 0, 0)) if per_batch else (lambda i, j: (0, 0, 0))),
                  pl.BlockSpec((1, d), lambda i, j: (0, 0)),
                  pl.BlockSpec(w_bf16.shape, lambda i, j: (0, 0))],
        out_specs=[o[0] for o in outs],
        out_shape=[o[1] for o in outs],
        compiler_params=_params("parallel", "parallel"),
        name="in_proj",
    )(x, mod, g, w_bf16)


CONV_TB = 32
CONV_PAD = 16
CONV_NORM_TB = 128


def _shifted_tiles(tiles, off, count, sub):
    a, r = divmod(off, SUBLANES)
    if r == 0:
        return tiles[a:a + count]
    rolled = [pltpu.roll(t, SUBLANES - r, 0) for t in tiles[a:a + count + 1]]
    keep = sub < SUBLANES - r
    return [jnp.where(keep, rolled[i], rolled[i + 1]) for i in range(count)]


def _conformer_kernel(u_ref, w_ref, b_ref, lg_ref, lb_ref, o_ref, pad_ref, w8_ref, conv_ref, *, n):
    c = u_ref.shape[-1]
    zeros = jnp.zeros((CONV_PAD, c), F32)
    pad_ref[0:CONV_PAD, :] = zeros
    pad_ref[n + CONV_PAD:n + 2 * CONV_PAD, :] = zeros
    pad_ref[CONV_PAD:n + CONV_PAD, :] = u_ref[0]
    for k in range(CONV_WIDTH):
        w8_ref[k] = jnp.broadcast_to(w_ref[k:k + 1, :], (SUBLANES, c))
    sub = lax.broadcasted_iota(jnp.int32, (SUBLANES, c), 0)
    bias = jnp.broadcast_to(b_ref[...], (SUBLANES, c))
    half = CONV_WIDTH // 2
    out_tiles = CONV_TB // SUBLANES
    win_tiles = (CONV_TB + 2 * CONV_PAD) // SUBLANES

    def block(i, carry):
        start = pl.multiple_of(i * CONV_TB, CONV_TB)
        tiles = [pad_ref[pl.ds(start + SUBLANES * m, SUBLANES), :] for m in range(win_tiles)]
        acc = [bias] * out_tiles
        for r in range(SUBLANES):
            sh = _shifted_tiles(tiles, r, win_tiles - 1, sub)
            for a in range(win_tiles - out_tiles):
                k = r + SUBLANES * a - (CONV_PAD - half)
                if 0 <= k < CONV_WIDTH:
                    w = w8_ref[k]
                    acc = [acc[j] + w * sh[j + a] for j in range(out_tiles)]
        conv_ref[pl.ds(start, CONV_TB), :] = jnp.concatenate(acc, axis=0)
        return carry

    lax.fori_loop(0, n // CONV_TB, block, 0)

    def norm_block(i, carry):
        start = pl.multiple_of(i * CONV_NORM_TB, CONV_NORM_TB)
        conv = conv_ref[pl.ds(start, CONV_NORM_TB), :]
        mu = jnp.mean(conv, axis=-1, keepdims=True)
        cen = conv - mu
        var = jnp.mean(cen * cen, axis=-1, keepdims=True)
        y = cen * lax.rsqrt(var + EPS) * lg_ref[...] + lb_ref[...]
        o_ref[0, pl.ds(start, CONV_NORM_TB), :] = (y * _sigmoid(y)).astype(o_ref.dtype)
        return carry

    lax.fori_loop(0, n // CONV_NORM_TB, norm_block, 0)


def _conformer(u, w, b, ln_g, ln_b):
    bsz, n, c = u.shape
    row = pl.BlockSpec((1, c), lambda i: (0, 0))
    return pl.pallas_call(
        functools.partial(_conformer_kernel, n=n),
        grid=(bsz,),
        in_specs=[pl.BlockSpec((1, n, c), lambda i: (i, 0, 0)),
                  pl.BlockSpec(w.shape, lambda i: (0, 0)), row, row, row],
        out_specs=pl.BlockSpec((1, n, c), lambda i: (i, 0, 0)),
        out_shape=jax.ShapeDtypeStruct((bsz, n, c), BF16),
        scratch_shapes=[pltpu.VMEM((n + 2 * CONV_PAD, c), F32), pltpu.VMEM((CONV_WIDTH, SUBLANES, c), F32),
                        pltpu.VMEM((n, c), F32)],
        compiler_params=_params("parallel"),
        name="conformer_conv",
    )(u, w, b.reshape(1, c), ln_g.reshape(1, c), ln_b.reshape(1, c))


def _head_of_lane():
    return lax.broadcasted_iota(jnp.int32, (1, HEAD_LANES), 1) // NA_HEAD_DIM


def _stack_heads(qr, head):
    zero = jnp.zeros_like(qr)
    return jnp.concatenate([jnp.where(head == h, qr, zero) for h in range(HEADS_PER_STEP)], axis=0)


def _unstack_heads(o, head, m):
    out = o[0:m]
    for h in range(1, HEADS_PER_STEP):
        out = jnp.where(head == h, o[h * m:(h + 1) * m], out)
    return out


def _build_na_bias(rpb_ref, bias_ref, group):
    qi = lax.broadcasted_iota(jnp.int32, (GRID_W, GRID_W), 0)
    ki = lax.broadcasted_iota(jnp.int32, (GRID_W, GRID_W), 1)
    rel = ki - qi
    col_start = jnp.clip(qi - NA_COLS // 2, 0, GRID_W - NA_COLS)
    valid = (ki >= col_start) & (ki < col_start + NA_COLS)
    n_r = 2 * NA_ROWS - 1
    n_c = 2 * NA_COLS - 1

    def per_head(h, carry):
        q0 = pl.multiple_of(h * GRID_W, GRID_W)
        for ri in range(n_r):
            base = ((group * HEADS_PER_STEP + h) * n_r + ri) * n_c
            t = jnp.full((GRID_W, GRID_W), NEG_INF, F32)
            for dc in range(n_c):
                t = jnp.where(rel == dc - (NA_COLS - 1), rpb_ref[base + dc], t)
            t = jnp.where(valid, t, NEG_INF)
            for d in range(NA_ROWS):
                j = ri + d - (NA_ROWS - 1)
                if 0 <= j < NA_ROWS:
                    bias_ref[d, pl.ds(q0, GRID_W), j * GRID_W:(j + 1) * GRID_W] = t
        return carry

    lax.fori_loop(0, HEADS_PER_STEP, per_head, 0)


def _na_latent_kernel(rpb_ref, q_ref, k_ref, v_ref, kc_ref, vc_ref, o_ref, bias_ref, *, rows):
    @pl.when(pl.program_id(1) == 0)
    def _():
        _build_na_bias(rpb_ref, bias_ref, pl.program_id(0))

    head = _head_of_lane()
    kc = kc_ref[0]
    vc = vc_ref[0]
    band = NA_ROWS * GRID_W
    nt = (((1,), (1,)), ((), ()))

    def row_step(r, carry):
        rs = jnp.clip(r - NA_ROWS // 2, 0, rows - NA_ROWS)
        q0 = pl.multiple_of(r * GRID_W, GRID_W)
        k0 = pl.multiple_of(rs * GRID_W, GRID_W)
        qs = _stack_heads(q_ref[0, pl.ds(q0, GRID_W), :], head)
        kb = k_ref[0, pl.ds(k0, band), :]
        vb = v_ref[0, pl.ds(k0, band), :]
        s_loc = lax.dot_general(qs, kb, nt, preferred_element_type=F32) + bias_ref[r - rs]
        s_ctx = lax.dot_general(qs, kc, nt, preferred_element_type=F32)
        m = jnp.maximum(jnp.max(s_loc, axis=-1, keepdims=True), jnp.max(s_ctx, axis=-1, keepdims=True))
        p_loc = jnp.exp(s_loc - m)
        p_ctx = jnp.exp(s_ctx - m)
        denom = jnp.sum(p_loc, axis=-1, keepdims=True) + jnp.sum(p_ctx, axis=-1, keepdims=True)
        o = (jnp.dot(p_loc.astype(BF16), vb, preferred_element_type=F32)
             + jnp.dot(p_ctx.astype(BF16), vc, preferred_element_type=F32))
        o = o * (1.0 / denom)
        o_ref[0, pl.ds(q0, GRID_W), :] = _unstack_heads(o, head, GRID_W).astype(o_ref.dtype)
        return carry

    lax.fori_loop(0, rows, row_step, 0, unroll=2)


def _na_latent(q, k, v, kc, vc, rpb):
    bsz, n, na_dim = q.shape
    nc = kc.shape[1]
    groups = na_dim // HEAD_LANES
    rows = n // GRID_W
    assert rows >= NA_ROWS and rows % 2 == 0
    lat = pl.BlockSpec((1, n, HEAD_LANES), lambda g, i: (i, 0, g))
    ctx = pl.BlockSpec((1, nc, HEAD_LANES), lambda g, i: (i, 0, g))
    return pl.pallas_call(
        functools.partial(_na_latent_kernel, rows=rows),
        grid=(groups, bsz),
        in_specs=[pl.BlockSpec(memory_space=pltpu.SMEM), lat, lat, lat, ctx, ctx],
        out_specs=lat,
        out_shape=jax.ShapeDtypeStruct((bsz, n, na_dim), BF16),
        scratch_shapes=[pltpu.VMEM((NA_ROWS, HEADS_PER_STEP * GRID_W, NA_ROWS * GRID_W), F32)],
        compiler_params=_params("arbitrary", "arbitrary"),
        name="na_latent",
    )(rpb.reshape(-1), q, k, v, kc, vc)


def _na_context_kernel(q_ref, k_ref, v_ref, o_ref):
    head = _head_of_lane()
    n = q_ref.shape[1]
    qs = _stack_heads(q_ref[0], head)
    s = lax.dot_general(qs, k_ref[0], (((1,), (1,)), ((), ())), preferred_element_type=F32)
    p = jnp.exp(s - jnp.max(s, axis=-1, keepdims=True))
    denom = jnp.sum(p, axis=-1, keepdims=True)
    o = jnp.dot(p.astype(BF16), v_ref[0], preferred_element_type=F32) * (1.0 / denom)
    o_ref[0] = _unstack_heads(o, head, n).astype(o_ref.dtype)


def _na_context(q, k, v):
    bsz, n, na_dim = q.shape
    blk = pl.BlockSpec((1, n, HEAD_LANES), lambda i, g: (i, 0, g))
    return pl.pallas_call(
        _na_context_kernel,
        grid=(bsz, na_dim // HEAD_LANES),
        in_specs=[blk, blk, blk],
        out_specs=blk,
        out_shape=jax.ShapeDtypeStruct((bsz, n, na_dim), BF16),
        compiler_params=_params("parallel", "parallel"),
        name="na_context",
    )(q, k, v)


LRU_TB = 256
LRU_PAD = 8


def _rglru_kernel(xl_ref, gl_ref, xc_ref, gc_ref, cw_ref, vec_ref, wx_ref, wa_ref, yl_ref, yc_ref,
                  padl_ref, padc_ref, a_ref, b_ref, h_ref, *, n_lat, n_ctx):
    c = xl_ref.shape[-1]
    zeros = jnp.zeros((LRU_PAD, c), F32)
    for pad_ref, src_ref, n in ((padc_ref, xc_ref, n_ctx), (padl_ref, xl_ref, n_lat)):
        pad_ref[0:LRU_PAD, :] = zeros
        pad_ref[n + LRU_PAD:n + 2 * LRU_PAD, :] = zeros
        pad_ref[LRU_PAD:n + LRU_PAD, :] = src_ref[0]
    sub = lax.broadcasted_iota(jnp.int32, (SUBLANES, c), 0)

    for direction in range(2):
        reverse = direction == 1
        conv_b = vec_ref[direction, 0:1, :]
        bx = vec_ref[direction, 1:2, :]
        ba = vec_ref[direction, 2:3, :]
        neg_lam = -vec_ref[direction, 3:4, :]
        softplus = jnp.maximum(neg_lam, 0.0) + jnp.log(1.0 + jnp.exp(-jnp.abs(neg_lam)))
        decay = -LRU_C * softplus

        conv_b8 = jnp.broadcast_to(conv_b, (SUBLANES, c))
        cw8 = [jnp.broadcast_to(cw_ref[direction, k:k + 1, :], (SUBLANES, c)) for k in range(LRU_CONV)]

        def coeffs(pad_ref, base, n):
            tb = min(n, LRU_TB)
            nt = tb // SUBLANES

            def block(i, carry):
                start = pl.multiple_of(i * tb, tb)
                tiles = [pad_ref[pl.ds(start + SUBLANES * m, SUBLANES), :]
                         for m in range(nt + 2 * LRU_PAD // SUBLANES)]
                u_tiles = [conv_b8] * nt
                for k in range(LRU_CONV):
                    off = LRU_PAD + k - (0 if reverse else LRU_CONV - 1)
                    sh = _shifted_tiles(tiles, off, nt, sub)
                    u_tiles = [u_tiles[j] + cw8[k] * sh[j] for j in range(nt)]
                u = jnp.concatenate(u_tiles, axis=0)
                ub = u.astype(BF16)
                gx = _sigmoid(jnp.dot(ub, wx_ref[direction], preferred_element_type=F32) + bx)
                ga = _sigmoid(jnp.dot(ub, wa_ref[direction], preferred_element_type=F32) + ba)
                a = jnp.exp(decay * ga)
                rem = 1.0 - a * a
                coef = jnp.where(rem > 0.0, rem * lax.rsqrt(rem), 0.0)
                a_ref[pl.ds(base + start, tb), :] = a
                b_ref[pl.ds(base + start, tb), :] = coef * (gx * u)
                return carry

            lax.fori_loop(0, n // tb, block, 0)

        coeffs(padc_ref, 0, n_ctx)
        coeffs(padl_ref, n_ctx, n_lat)

        def scan(base, n, hb):
            chunks = n // SUBLANES

            def chunk(j, hb):
                jj = chunks - 1 - j if reverse else j
                rows = pl.ds(pl.multiple_of(base + jj * SUBLANES, SUBLANES), SUBLANES)
                a = a_ref[rows, :]
                b = b_ref[rows, :]
                for d in (1, 2, 4):
                    shift = SUBLANES - d if reverse else d
                    keep = (sub < SUBLANES - d) if reverse else (sub >= d)
                    ra = pltpu.roll(a, shift, 0)
                    rb = pltpu.roll(b, shift, 0)
                    b = jnp.where(keep, a * rb + b, b)
                    a = jnp.where(keep, a * ra, a)
                h = a * hb + b
                if reverse:
                    h_ref[rows, :] = h_ref[rows, :] + h
                else:
                    h_ref[rows, :] = h
                last = 0 if reverse else SUBLANES - 1
                a_last = jnp.broadcast_to(a[last:last + 1, :], (SUBLANES, c))
                b_last = jnp.broadcast_to(b[last:last + 1, :], (SUBLANES, c))
                return a_last * hb + b_last

            return lax.fori_loop(0, chunks, chunk, hb, unroll=8)

        hb = scan(0, n_ctx, jnp.zeros((SUBLANES, c), F32))
        scan(n_ctx, n_lat, hb)

    def gate_out(g_ref, y_ref, base, n):
        tb = min(n, LRU_TB)

        def block(i, carry):
            start = pl.multiple_of(i * tb, tb)
            g = g_ref[0, pl.ds(start, tb), :]
            y_ref[0, pl.ds(start, tb), :] = (_gelu_tanh(g) * h_ref[pl.ds(base + start, tb), :]).astype(y_ref.dtype)
            return carry

        lax.fori_loop(0, n // tb, block, 0)

    gate_out(gc_ref, yc_ref, 0, n_ctx)
    gate_out(gl_ref, yl_ref, n_ctx, n_lat)


def _block_diag(w):
    two, nb, m, _ = w.shape
    eye = jnp.eye(nb, dtype=w.dtype)
    return (w[:, :, :, None, :] * eye[None, :, None, :, None]).reshape(two, nb * m, nb * m)


def _rglru(xl, gl, xc, gc, conv_w, conv_b, wx, bx, wa, ba, lam):
    bsz, n_lat, c = xl.shape
    n_ctx = xc.shape[1]
    vec = jnp.stack([conv_b, bx, ba, lam], axis=1)
    wxd = _block_diag(wx).astype(BF16)
    wad = _block_diag(wa).astype(BF16)
    lat = pl.BlockSpec((1, n_lat, c), lambda i: (i, 0, 0))
    ctx = pl.BlockSpec((1, n_ctx, c), lambda i: (i, 0, 0))
    full = lambda a: pl.BlockSpec(a.shape, lambda i: (0,) * a.ndim)
    return pl.pallas_call(
        functools.partial(_rglru_kernel, n_lat=n_lat, n_ctx=n_ctx),
        grid=(bsz,),
        in_specs=[lat, lat, ctx, ctx, full(conv_w), full(vec), full(wxd), full(wad)],
        out_specs=[lat, ctx],
        out_shape=[jax.ShapeDtypeStruct((bsz, n_lat, c), BF16), jax.ShapeDtypeStruct((bsz, n_ctx, c), BF16)],
        scratch_shapes=[pltpu.VMEM((n_lat + 2 * LRU_PAD, c), F32), pltpu.VMEM((n_ctx + 2 * LRU_PAD, c), F32),
                        pltpu.VMEM((n_ctx + n_lat, c), F32), pltpu.VMEM((n_ctx + n_lat, c), F32),
                        pltpu.VMEM((n_ctx + n_lat, c), F32)],
        compiler_params=_params("parallel"),
        name="rglru",
    )(xl, gl, xc, gc, conv_w, vec, wxd, wad)


FF_CHUNK = 1024


def _out_mlp_kernel(x_ref, ya_ref, yb_ref, yc_ref, mod_ref, g2_ref, gf_ref, wo_ref, w1_ref, w2_ref, o_ref, *,
                    final_norm):
    ca = ya_ref.shape[-1]
    cb = yb_ref.shape[-1]
    cc = yc_ref.shape[-1]
    mix = (jnp.dot(ya_ref[0], wo_ref[0:ca, :], preferred_element_type=F32)
           + jnp.dot(yb_ref[0], wo_ref[ca:ca + cb, :], preferred_element_type=F32)
           + jnp.dot(yc_ref[0], wo_ref[ca + cb:ca + cb + cc, :], preferred_element_type=F32))
    x1 = x_ref[0] + mod_ref[0, 2:3, :] * mix
    h2 = _rms(x1) * g2_ref[...]
    h2 = (h2 * (1.0 + mod_ref[0, 4:5, :]) + mod_ref[0, 3:4, :]).astype(BF16)
    acc = jnp.zeros(x1.shape, F32)
    for j in range(w1_ref.shape[1] // FF_CHUNK):
        cols = slice(j * FF_CHUNK, (j + 1) * FF_CHUNK)
        hid = jnp.maximum(jnp.dot(h2, w1_ref[:, cols], preferred_element_type=F32), 0.0)
        acc = acc + jnp.dot((hid * hid).astype(BF16), w2_ref[cols, :], preferred_element_type=F32)
    out = x1 + mod_ref[0, 5:6, :] * acc
    if final_norm:
        out = _rms(out) * gf_ref[...]
    o_ref[0] = out


def _out_mlp(x, ya, yb, yc, mod, g2, gf, wo, w1, w2, *, final_norm):
    b, n, d = x.shape
    tm = min(n, 512)
    per_batch = mod.shape[0] != 1
    tok = lambda a: pl.BlockSpec((1, tm, a.shape[-1]), lambda i, j: (i, j, 0))
    resident = lambda a: pl.BlockSpec(a.shape, lambda i, j: (0, 0), pipeline_mode=pl.Buffered(1))
    return pl.pallas_call(
        functools.partial(_out_mlp_kernel, final_norm=final_norm),
        grid=(b, n // tm),
        in_specs=[tok(x), tok(ya), tok(yb), tok(yc),
                  pl.BlockSpec((1, 6, d), (lambda i, j: (i, 0, 0)) if per_batch else (lambda i, j: (0, 0, 0))),
                  pl.BlockSpec((1, d), lambda i, j: (0, 0)), pl.BlockSpec((1, d), lambda i, j: (0, 0)),
                  resident(wo), resident(w1), resident(w2)],
        out_specs=tok(x),
        out_shape=jax.ShapeDtypeStruct((b, n, d), F32),
        compiler_params=_params("parallel", "parallel"),
        name="out_proj_mlp",
    )(x, ya, yb, yc, mod, g2, gf, wo, w1, w2)


def kernel(x, c, ctx, c_ctx, norm1_g, norm2_g, ada_w, ada_b, w_in, w_out, conv_w, conv_b, conv_ln_g, conv_ln_b,
           na_rpb, lru_conv_w, lru_conv_b, lru_wx, lru_bx, lru_wa, lru_ba, lru_lambda, mlp_w1, mlp_w2, final_g):
    depth = w_in.shape[0]
    bsz, _, d = x.shape
    conv_dim = conv_w.shape[-1]
    lru_dim = lru_conv_w.shape[-1]
    na_dim = NA_HEADS * NA_HEAD_DIM
    dims = dict(conv_dim=conv_dim, na_dim=na_dim, lru_dim=lru_dim)

    cond = jnp.concatenate([c, c_ctx[None, :]], axis=0)
    cond = jnp.pad(cond, ((0, -(bsz + 1) % SUBLANES), (0, 0)))
    mod_all = _modulation(cond, ada_w, ada_b)
    cx = ctx
    for l in range(depth):
        last = l == depth - 1
        mod = mod_all[l, :bsz].reshape(bsz, 6, d)
        mod_c = mod_all[l, bsz:bsz + 1].reshape(1, 6, d)
        w_in_l = w_in[l].astype(BF16)
        g1 = norm1_g[l].reshape(1, d)

        u, q, k, v, rx, rg = _in_proj(x, mod, g1, w_in_l, **dims)
        cu, cq, ck, cv, crx, crg = _in_proj(cx, mod_c, g1, w_in_l, **dims)

        y_a = _conformer(u, conv_w[l], conv_b[l], conv_ln_g[l], conv_ln_b[l])
        y_b = _na_latent(q, k, v, ck, cv, na_rpb[l])
        y_c, yc_c = _rglru(rx, rg, crx, crg, lru_conv_w[l], lru_conv_b[l], lru_wx[l], lru_bx[l],
                           lru_wa[l], lru_ba[l], lru_lambda[l])

        w_out_l = w_out[l].astype(BF16)
        w1_l = mlp_w1[l].astype(BF16)
        w2_l = mlp_w2[l].astype(BF16)
        g2 = norm2_g[l].reshape(1, d)
        gf = final_g.reshape(1, d)
        x = _out_mlp(x, y_a, y_b, y_c, mod, g2, gf, w_out_l, w1_l, w2_l, final_norm=last)
        if not last:
            yc_a = _conformer(cu, conv_w[l], conv_b[l], conv_ln_g[l], conv_ln_b[l])
            yc_b = _na_context(cq, ck, cv)
            cx = _out_mlp(cx, yc_a, yc_b, yc_c, mod_c, g2, gf, w_out_l, w1_l, w2_l, final_norm=False)
    return x
```

```python
import functools

import jax
import jax.numpy as jnp
from jax import lax
from jax.experimental import pallas as pl
from jax.experimental.pallas import tpu as pltpu

GRID_W = 64
CONV_WIDTH = 31
NA_HEADS = 8
NA_HEAD_DIM = 64
NA_ROWS = 8
NA_COLS = 16
LRU_BLOCKS = 4
LRU_CONV = 4
LRU_C = 8.0
EPS = 1e-6
NEG_INF = -1e30

SUBLANES = 8
LANES = 128
HEADS_PER_STEP = 4
HEAD_LANES = HEADS_PER_STEP * NA_HEAD_DIM
VMEM_LIMIT_BYTES = 56 * 1024 * 1024

F32 = jnp.float32
BF16 = jnp.bfloat16


def _sigmoid(x):
    return 1.0 / (1.0 + jnp.exp(-x))


def _gelu_tanh(x):
    return 0.5 * x * (1.0 + jnp.tanh(0.7978845608028654 * (x + 0.044715 * (x * x * x))))


def _rms(x):
    return x * lax.rsqrt(jnp.mean(x * x, axis=-1, keepdims=True) + EPS)


def _params(*sem):
    return pltpu.CompilerParams(dimension_semantics=sem, vmem_limit_bytes=VMEM_LIMIT_BYTES)


def _mod_kernel(c_ref, w_ref, b_ref, o_ref):
    c = c_ref[...]
    s = c * _sigmoid(c)
    o_ref[0] = jnp.dot(s, w_ref[0], preferred_element_type=F32,
                       precision=lax.Precision.HIGHEST) + b_ref[0]


def _modulation(cc, ada_w, ada_b):
    depth, d, n = ada_w.shape
    r = cc.shape[0]
    tn = n // 4
    return pl.pallas_call(
        _mod_kernel,
        grid=(depth, n // tn),
        in_specs=[pl.BlockSpec((r, d), lambda l, j: (0, 0)),
                  pl.BlockSpec((1, d, tn), lambda l, j: (l, 0, j)),
                  pl.BlockSpec((1, 1, tn), lambda l, j: (l, 0, j))],
        out_specs=pl.BlockSpec((1, r, tn), lambda l, j: (l, 0, j)),
        out_shape=jax.ShapeDtypeStruct((depth, r, n), F32),
        compiler_params=_params("parallel", "parallel"),
        name="adaln_modulation",
    )(cc, ada_w, ada_b.reshape(depth, 1, n))


def _in_proj_kernel(x_ref, mod_ref, g_ref, w_ref, u_ref, q_ref, k_ref, v_ref, rx_ref, rg_ref, *,
                    conv_dim, na_dim, lru_dim):
    x = x_ref[0]
    h = _rms(x) * g_ref[...]
    h = h * (1.0 + mod_ref[0, 1:2, :]) + mod_ref[0, 0:1, :]
    hb = h.astype(BF16)

    def proj(off, width):
        return jnp.dot(hb, w_ref[:, off:off + width], preferred_element_type=F32)

    off = 0
    val = proj(off, conv_dim); off += conv_dim
    gate = proj(off, conv_dim); off += conv_dim
    u_ref[0] = val * _sigmoid(gate)
    q_ref[0] = (proj(off, na_dim) * (NA_HEAD_DIM ** -0.5)).astype(BF16); off += na_dim
    k_ref[0] = proj(off, na_dim).astype(BF16); off += na_dim
    v_ref[0] = proj(off, na_dim).astype(BF16); off += na_dim
    rx_ref[0] = proj(off, lru_dim); off += lru_dim
    rg_ref[0] = proj(off, lru_dim)


IN_PROJ_TM = 1024


def _in_proj(x, mod, g, w_bf16, *, conv_dim, na_dim, lru_dim):
    b, n, d = x.shape
    tm = min(n, IN_PROJ_TM)
    per_batch = mod.shape[0] != 1
    tok = lambda width, dt: (pl.BlockSpec((1, tm, width), lambda i, j: (i, j, 0)),
                             jax.ShapeDtypeStruct((b, n, width), dt))
    outs = [tok(conv_dim, F32), tok(na_dim, BF16), tok(na_dim, BF16), tok(na_dim, BF16),
            tok(lru_dim, F32), tok(lru_dim, F32)]
    return pl.pallas_call(
        functools.partial(_in_proj_kernel, conv_dim=conv_dim, na_dim=na_dim, lru_dim=lru_dim),
        grid=(b, n // tm),
        in_specs=[pl.BlockSpec((1, tm, d), lambda i, j: (i, j, 0)),
                  pl.BlockSpec((1, 6, d), (lambda i, j: (i, 0, 0)) if per_batch else (lambda i, j: (0, 0, 0))),
                  pl.BlockSpec((1, d), lambda i, j: (0, 0)),
                  pl.BlockSpec(w_bf16.shape, lambda i, j: (0, 0))],
        out_specs=[o[0] for o in outs],
        out_shape=[o[1] for o in outs],
        compiler_params=_params("parallel", "parallel"),
        name="in_proj",
    )(x, mod, g, w_bf16)


CONV_TB = 32
CONV_PAD = 16
CONV_NORM_TB = 128


def _shifted_tiles(tiles, off, count, sub):
    a, r = divmod(off, SUBLANES)
    if r == 0:
        return tiles[a:a + count]
    rolled = [pltpu.roll(t, SUBLANES - r, 0) for t in tiles[a:a + count + 1]]
    keep = sub < SUBLANES - r
    return [jnp.where(keep, rolled[i], rolled[i + 1]) for i in range(count)]


def _conformer_kernel(u_ref, w_ref, b_ref, lg_ref, lb_ref, o_ref, pad_ref, w8_ref, conv_ref, *, n):
    c = u_ref.shape[-1]
    zeros = jnp.zeros((CONV_PAD, c), F32)
    pad_ref[0:CONV_PAD, :] = zeros
    pad_ref[n + CONV_PAD:n + 2 * CONV_PAD, :] = zeros
    pad_ref[CONV_PAD:n + CONV_PAD, :] = u_ref[0]
    for k in range(CONV_WIDTH):
        w8_ref[k] = jnp.broadcast_to(w_ref[k:k + 1, :], (SUBLANES, c))
    sub = lax.broadcasted_iota(jnp.int32, (SUBLANES, c), 0)
    bias = jnp.broadcast_to(b_ref[...], (SUBLANES, c))
    half = CONV_WIDTH // 2
    out_tiles = CONV_TB // SUBLANES
    win_tiles = (CONV_TB + 2 * CONV_PAD) // SUBLANES

    def block(i, carry):
        start = pl.multiple_of(i * CONV_TB, CONV_TB)
        tiles = [pad_ref[pl.ds(start + SUBLANES * m, SUBLANES), :] for m in range(win_tiles)]
        acc = [bias] * out_tiles
        for r in range(SUBLANES):
            sh = _shifted_tiles(tiles, r, win_tiles - 1, sub)
            for a in range(win_tiles - out_tiles):
                k = r + SUBLANES * a - (CONV_PAD - half)
                if 0 <= k < CONV_WIDTH:
                    w = w8_ref[k]
                    acc = [acc[j] + w * sh[j + a] for j in range(out_tiles)]
        conv_ref[pl.ds(start, CONV_TB), :] = jnp.concatenate(acc, axis=0)
        return carry

    lax.fori_loop(0, n // CONV_TB, block, 0)

    def norm_block(i, carry):
        start = pl.multiple_of(i * CONV_NORM_TB, CONV_NORM_TB)
        conv = conv_ref[pl.ds(start, CONV_NORM_TB), :]
        mu = jnp.mean(conv, axis=-1, keepdims=True)
        cen = conv - mu
        var = jnp.mean(cen * cen, axis=-1, keepdims=True)
        y = cen * lax.rsqrt(var + EPS) * lg_ref[...] + lb_ref[...]
        o_ref[0, pl.ds(start, CONV_NORM_TB), :] = (y * _sigmoid(y)).astype(o_ref.dtype)
        return carry

    lax.fori_loop(0, n // CONV_NORM_TB, norm_block, 0, unroll=4)


def _conformer(u, w, b, ln_g, ln_b):
    bsz, n, c = u.shape
    row = pl.BlockSpec((1, c), lambda i: (0, 0))
    return pl.pallas_call(
        functools.partial(_conformer_kernel, n=n),
        grid=(bsz,),
        in_specs=[pl.BlockSpec((1, n, c), lambda i: (i, 0, 0)),
                  pl.BlockSpec(w.shape, lambda i: (0, 0)), row, row, row],
        out_specs=pl.BlockSpec((1, n, c), lambda i: (i, 0, 0)),
        out_shape=jax.ShapeDtypeStruct((bsz, n, c), BF16),
        scratch_shapes=[pltpu.VMEM((n + 2 * CONV_PAD, c), F32), pltpu.VMEM((CONV_WIDTH, SUBLANES, c), F32),
                        pltpu.VMEM((n, c), F32)],
        compiler_params=_params("parallel"),
        name="conformer_conv",
    )(u, w, b.reshape(1, c), ln_g.reshape(1, c), ln_b.reshape(1, c))


def _head_of_lane():
    return lax.broadcasted_iota(jnp.int32, (1, HEAD_LANES), 1) // NA_HEAD_DIM


def _stack_heads(qr, head):
    zero = jnp.zeros_like(qr)
    return jnp.concatenate([jnp.where(head == h, qr, zero) for h in range(HEADS_PER_STEP)], axis=0)


def _unstack_heads(o, head, m):
    out = o[0:m]
    for h in range(1, HEADS_PER_STEP):
        out = jnp.where(head == h, o[h * m:(h + 1) * m], out)
    return out


NA_PIPE_UNROLL = 2


def _build_na_bias(rpb_ref, bias_ref, group):
    qi = lax.broadcasted_iota(jnp.int32, (GRID_W, GRID_W), 0)
    ki = lax.broadcasted_iota(jnp.int32, (GRID_W, GRID_W), 1)
    rel = ki - qi
    col_start = jnp.clip(qi - NA_COLS // 2, 0, GRID_W - NA_COLS)
    valid = (ki >= col_start) & (ki < col_start + NA_COLS)
    n_r = 2 * NA_ROWS - 1
    n_c = 2 * NA_COLS - 1

    def per_head(h, carry):
        q0 = pl.multiple_of(h * GRID_W, GRID_W)
        for ri in range(n_r):
            base = ((group * HEADS_PER_STEP + h) * n_r + ri) * n_c
            t = jnp.full((GRID_W, GRID_W), NEG_INF, F32)
            for dc in range(n_c):
                t = jnp.where(rel == dc - (NA_COLS - 1), rpb_ref[base + dc], t)
            t = jnp.where(valid, t, NEG_INF)
            for d in range(NA_ROWS):
                j = ri + d - (NA_ROWS - 1)
                if 0 <= j < NA_ROWS:
                    bias_ref[d, pl.ds(q0, GRID_W), j * GRID_W:(j + 1) * GRID_W] = t
        return carry

    lax.fori_loop(0, HEADS_PER_STEP, per_head, 0)


def _lane_reduce(tiles, combine, reduce):
    by_width = {}
    for t in tiles:
        w = t.shape[-1]
        by_width[w] = t if w not in by_width else combine(by_width[w], t)
    parts = [reduce(t, axis=-1, keepdims=True) for t in by_width.values()]
    out = parts[0]
    for part in parts[1:]:
        out = combine(out, part)
    return out


def _na_latent_kernel(rpb_ref, q_ref, k_ref, v_ref, kc_ref, vc_ref, o_ref, bias_ref,
                      s0_ref, s1_ref, p0_ref, p1_ref, inv0_ref, inv1_ref, *, rows):
    @pl.when(pl.program_id(1) == 0)
    def _():
        _build_na_bias(rpb_ref, bias_ref, pl.program_id(0))

    head = _head_of_lane()
    band = NA_ROWS * GRID_W
    n_keys = s0_ref.shape[1]
    nt = (((1,), (1,)), ((), ()))
    s_refs = (s0_ref, s1_ref)
    p_refs = (p0_ref, p1_ref)
    inv_refs = (inv0_ref, inv1_ref)
    bounds = list(range(0, band, HEAD_LANES)) + [band, n_keys]
    spans = list(zip(bounds[:-1], bounds[1:]))

    def band_start(r):
        return jnp.clip(r - NA_ROWS // 2, 0, rows - NA_ROWS)

    def scores(r, slot):
        rs = band_start(r)
        qs = _stack_heads(q_ref[0, pl.ds(pl.multiple_of(r * GRID_W, GRID_W), GRID_W), :], head)
        kb = k_ref[0, pl.ds(pl.multiple_of(rs * GRID_W, GRID_W), band), :]
        s_refs[slot][:, 0:band] = lax.dot_general(qs, kb, nt, preferred_element_type=F32) + bias_ref[r - rs]
        s_refs[slot][:, band:] = lax.dot_general(qs, kc_ref[0], nt, preferred_element_type=F32)

    def softmax(slot):
        tiles = [s_refs[slot][:, lo:hi] for lo, hi in spans]
        m = _lane_reduce(tiles, jnp.maximum, jnp.max)
        probs = [jnp.exp(t - m) for t in tiles]
        for (lo, hi), p in zip(spans, probs):
            p_refs[slot][:, lo:hi] = p.astype(BF16)
        inv_refs[slot][...] = 1.0 / _lane_reduce(probs, jnp.add, jnp.sum)

    def values(r, slot):
        rs = band_start(r)
        vb = v_ref[0, pl.ds(pl.multiple_of(rs * GRID_W, GRID_W), band), :]
        o = jnp.dot(p_refs[slot][:, 0:band], vb, preferred_element_type=F32)
        o = o + jnp.dot(p_refs[slot][:, band:], vc_ref[0], preferred_element_type=F32)
        o = o * inv_refs[slot][...]
        q0 = pl.multiple_of(r * GRID_W, GRID_W)
        o_ref[0, pl.ds(q0, GRID_W), :] = _unstack_heads(o, head, GRID_W).astype(o_ref.dtype)

    def step(i, parity, do_scores=True, do_values=True, do_softmax=True):
        if do_scores:
            scores(i, parity)
        if do_values:
            values(i - 2, parity)
        if do_softmax:
            softmax(1 - parity)

    step(0, 0, do_values=False, do_softmax=False)
    step(1, 1, do_values=False)

    def steady(j, carry):
        base = NA_PIPE_UNROLL * (j + 1)
        for t in range(NA_PIPE_UNROLL):
            step(base + t, t % 2)
        return carry

    lax.fori_loop(0, rows // NA_PIPE_UNROLL - 1, steady, 0)
    step(rows, 0, do_scores=False)
    step(rows + 1, 1, do_scores=False, do_softmax=False)


def _na_latent(q, k, v, kc, vc, rpb):
    bsz, n, na_dim = q.shape
    nc = kc.shape[1]
    groups = na_dim // HEAD_LANES
    rows = n // GRID_W
    assert rows >= NA_ROWS and rows % NA_PIPE_UNROLL == 0
    stacked = HEADS_PER_STEP * GRID_W
    n_keys = NA_ROWS * GRID_W + nc
    lat = pl.BlockSpec((1, n, HEAD_LANES), lambda g, i: (i, 0, g))
    ctx = pl.BlockSpec((1, nc, HEAD_LANES), lambda g, i: (i, 0, g))
    return pl.pallas_call(
        functools.partial(_na_latent_kernel, rows=rows),
        grid=(groups, bsz),
        in_specs=[pl.BlockSpec(memory_space=pltpu.SMEM), lat, lat, lat, ctx, ctx],
        out_specs=lat,
        out_shape=jax.ShapeDtypeStruct((bsz, n, na_dim), BF16),
        scratch_shapes=[pltpu.VMEM((NA_ROWS, stacked, NA_ROWS * GRID_W), F32)]
        + [pltpu.VMEM((stacked, n_keys), F32)] * 2
        + [pltpu.VMEM((stacked, n_keys), BF16)] * 2
        + [pltpu.VMEM((stacked, 1), F32)] * 2,
        compiler_params=_params("arbitrary", "arbitrary"),
        name="na_latent",
    )(rpb.reshape(-1), q, k, v, kc, vc)


def _na_context_kernel(q_ref, k_ref, v_ref, o_ref):
    head = _head_of_lane()
    n = q_ref.shape[1]
    qs = _stack_heads(q_ref[0], head)
    s = lax.dot_general(qs, k_ref[0], (((1,), (1,)), ((), ())), preferred_element_type=F32)
    p = jnp.exp(s - jnp.max(s, axis=-1, keepdims=True))
    denom = jnp.sum(p, axis=-1, keepdims=True)
    o = jnp.dot(p.astype(BF16), v_ref[0], preferred_element_type=F32) * (1.0 / denom)
    o_ref[0] = _unstack_heads(o, head, n).astype(o_ref.dtype)


def _na_context(q, k, v):
    bsz, n, na_dim = q.shape
    blk = pl.BlockSpec((1, n, HEAD_LANES), lambda i, g: (i, 0, g))
    return pl.pallas_call(
        _na_context_kernel,
        grid=(bsz, na_dim // HEAD_LANES),
        in_specs=[blk, blk, blk],
        out_specs=blk,
        out_shape=jax.ShapeDtypeStruct((bsz, n, na_dim), BF16),
        compiler_params=_params("parallel", "parallel"),
        name="na_context",
    )(q, k, v)


LRU_TB = 256
LRU_PAD = 8


def _rglru_kernel(xl_ref, gl_ref, xc_ref, gc_ref, cw_ref, vec_ref, wx_ref, wa_ref, yl_ref, yc_ref,
                  padl_ref, padc_ref, a_ref, b_ref, h_ref, *, n_lat, n_ctx):
    c = xl_ref.shape[-1]
    zeros = jnp.zeros((LRU_PAD, c), F32)
    for pad_ref, src_ref, n in ((padc_ref, xc_ref, n_ctx), (padl_ref, xl_ref, n_lat)):
        pad_ref[0:LRU_PAD, :] = zeros
        pad_ref[n + LRU_PAD:n + 2 * LRU_PAD, :] = zeros
        pad_ref[LRU_PAD:n + LRU_PAD, :] = src_ref[0]
    sub = lax.broadcasted_iota(jnp.int32, (SUBLANES, c), 0)

    for direction in range(2):
        reverse = direction == 1
        conv_b = vec_ref[direction, 0:1, :]
        bx = vec_ref[direction, 1:2, :]
        ba = vec_ref[direction, 2:3, :]
        neg_lam = -vec_ref[direction, 3:4, :]
        softplus = jnp.maximum(neg_lam, 0.0) + jnp.log(1.0 + jnp.exp(-jnp.abs(neg_lam)))
        decay = -LRU_C * softplus

        conv_b8 = jnp.broadcast_to(conv_b, (SUBLANES, c))
        cw8 = [jnp.broadcast_to(cw_ref[direction, k:k + 1, :], (SUBLANES, c)) for k in range(LRU_CONV)]

        def coeffs(pad_ref, base, n):
            tb = min(n, LRU_TB)
            nt = tb // SUBLANES

            def block(i, carry):
                start = pl.multiple_of(i * tb, tb)
                tiles = [pad_ref[pl.ds(start + SUBLANES * m, SUBLANES), :]
                         for m in range(nt + 2 * LRU_PAD // SUBLANES)]
                u_tiles = [conv_b8] * nt
                for k in range(LRU_CONV):
                    off = LRU_PAD + k - (0 if reverse else LRU_CONV - 1)
                    sh = _shifted_tiles(tiles, off, nt, sub)
                    u_tiles = [u_tiles[j] + cw8[k] * sh[j] for j in range(nt)]
                u = jnp.concatenate(u_tiles, axis=0)
                ub = u.astype(BF16)
                gx = _sigmoid(jnp.dot(ub, wx_ref[direction], preferred_element_type=F32) + bx)
                ga = _sigmoid(jnp.dot(ub, wa_ref[direction], preferred_element_type=F32) + ba)
                a = jnp.exp(decay * ga)
                rem = 1.0 - a * a
                coef = jnp.where(rem > 0.0, rem * lax.rsqrt(rem), 0.0)
                a_ref[pl.ds(base + start, tb), :] = a
                b_ref[pl.ds(base + start, tb), :] = coef * (gx * u)
                return carry

            lax.fori_loop(0, n // tb, block, 0)

        coeffs(padc_ref, 0, n_ctx)
        coeffs(padl_ref, n_ctx, n_lat)

        def scan(base, n, hb):
            chunks = n // SUBLANES

            def chunk(j, hb):
                jj = chunks - 1 - j if reverse else j
                rows = pl.ds(pl.multiple_of(base + jj * SUBLANES, SUBLANES), SUBLANES)
                a = a_ref[rows, :]
                b = b_ref[rows, :]
                for d in (1, 2, 4):
                    shift = SUBLANES - d if reverse else d
                    keep = (sub < SUBLANES - d) if reverse else (sub >= d)
                    ra = pltpu.roll(a, shift, 0)
                    rb = pltpu.roll(b, shift, 0)
                    b = jnp.where(keep, a * rb + b, b)
                    a = jnp.where(keep, a * ra, a)
                h = a * hb + b
                if reverse:
                    h_ref[rows, :] = h_ref[rows, :] + h
                else:
                    h_ref[rows, :] = h
                last = 0 if reverse else SUBLANES - 1
                a_last = jnp.broadcast_to(a[last:last + 1, :], (SUBLANES, c))
                b_last = jnp.broadcast_to(b[last:last + 1, :], (SUBLANES, c))
                return a_last * hb + b_last

            return lax.fori_loop(0, chunks, chunk, hb, unroll=8)

        hb = scan(0, n_ctx, jnp.zeros((SUBLANES, c), F32))
        scan(n_ctx, n_lat, hb)

    def gate_out(g_ref, y_ref, base, n):
        tb = min(n, LRU_TB)

        def block(i, carry):
            start = pl.multiple_of(i * tb, tb)
            g = g_ref[0, pl.ds(start, tb), :]
            y_ref[0, pl.ds(start, tb), :] = (_gelu_tanh(g) * h_ref[pl.ds(base + start, tb), :]).astype(y_ref.dtype)
            return carry

        lax.fori_loop(0, n // tb, block, 0)

    gate_out(gc_ref, yc_ref, 0, n_ctx)
    gate_out(gl_ref, yl_ref, n_ctx, n_lat)


def _block_diag(w):
    two, nb, m, _ = w.shape
    eye = jnp.eye(nb, dtype=w.dtype)
    return (w[:, :, :, None, :] * eye[None, :, None, :, None]).reshape(two, nb * m, nb * m)


def _rglru(xl, gl, xc, gc, conv_w, conv_b, wx, bx, wa, ba, lam):
    bsz, n_lat, c = xl.shape
    n_ctx = xc.shape[1]
    vec = jnp.stack([conv_b, bx, ba, lam], axis=1)
    wxd = _block_diag(wx).astype(BF16)
    wad = _block_diag(wa).astype(BF16)
    lat = pl.BlockSpec((1, n_lat, c), lambda i: (i, 0, 0))
    ctx = pl.BlockSpec((1, n_ctx, c), lambda i: (i, 0, 0))
    full = lambda a: pl.BlockSpec(a.shape, lambda i: (0,) * a.ndim)
    return pl.pallas_call(
        functools.partial(_rglru_kernel, n_lat=n_lat, n_ctx=n_ctx),
        grid=(bsz,),
        in_specs=[lat, lat, ctx, ctx, full(conv_w), full(vec), full(wxd), full(wad)],
        out_specs=[lat, ctx],
        out_shape=[jax.ShapeDtypeStruct((bsz, n_lat, c), BF16), jax.ShapeDtypeStruct((bsz, n_ctx, c), BF16)],
        scratch_shapes=[pltpu.VMEM((n_lat + 2 * LRU_PAD, c), F32), pltpu.VMEM((n_ctx + 2 * LRU_PAD, c), F32),
                        pltpu.VMEM((n_ctx + n_lat, c), F32), pltpu.VMEM((n_ctx + n_lat, c), F32),
                        pltpu.VMEM((n_ctx + n_lat, c), F32)],
        compiler_params=_params("parallel"),
        name="rglru",
    )(xl, gl, xc, gc, conv_w, vec, wxd, wad)


FF_CHUNK = 1024


def _out_mlp_kernel(x_ref, ya_ref, yb_ref, yc_ref, mod_ref, g2_ref, gf_ref, wo_ref, w1_ref, w2_ref, o_ref, *,
                    final_norm):
    ca = ya_ref.shape[-1]
    cb = yb_ref.shape[-1]
    cc = yc_ref.shape[-1]
    mix = (jnp.dot(ya_ref[0], wo_ref[0:ca, :], preferred_element_type=F32)
           + jnp.dot(yb_ref[0], wo_ref[ca:ca + cb, :], preferred_element_type=F32)
           + jnp.dot(yc_ref[0], wo_ref[ca + cb:ca + cb + cc, :], preferred_element_type=F32))
    x1 = x_ref[0] + mod_ref[0, 2:3, :] * mix
    h2 = _rms(x1) * g2_ref[...]
    h2 = (h2 * (1.0 + mod_ref[0, 4:5, :]) + mod_ref[0, 3:4, :]).astype(BF16)
    acc = jnp.zeros(x1.shape, F32)
    for j in range(w1_ref.shape[1] // FF_CHUNK):
        cols = slice(j * FF_CHUNK, (j + 1) * FF_CHUNK)
        hid = jnp.maximum(jnp.dot(h2, w1_ref[:, cols], preferred_element_type=F32), 0.0)
        acc = acc + jnp.dot((hid * hid).astype(BF16), w2_ref[cols, :], preferred_element_type=F32)
    out = x1 + mod_ref[0, 5:6, :] * acc
    if final_norm:
        out = _rms(out) * gf_ref[...]
    o_ref[0] = out


def _out_mlp(x, ya, yb, yc, mod, g2, gf, wo, w1, w2, *, final_norm):
    b, n, d = x.shape
    tm = min(n, 512)
    per_batch = mod.shape[0] != 1
    tok = lambda a: pl.BlockSpec((1, tm, a.shape[-1]), lambda i, j: (i, j, 0))
    resident = lambda a: pl.BlockSpec(a.shape, lambda i, j: (0, 0), pipeline_mode=pl.Buffered(1))
    return pl.pallas_call(
        functools.partial(_out_mlp_kernel, final_norm=final_norm),
        grid=(b, n // tm),
        in_specs=[tok(x), tok(ya), tok(yb), tok(yc),
                  pl.BlockSpec((1, 6, d), (lambda i, j: (i, 0, 0)) if per_batch else (lambda i, j: (0, 0, 0))),
                  pl.BlockSpec((1, d), lambda i, j: (0, 0)), pl.BlockSpec((1, d), lambda i, j: (0, 0)),
                  resident(wo), resident(w1), resident(w2)],
        out_specs=tok(x),
        out_shape=jax.ShapeDtypeStruct((b, n, d), F32),
        compiler_params=_params("parallel", "parallel"),
        name="out_proj_mlp",
    )(x, ya, yb, yc, mod, g2, gf, wo, w1, w2)


def kernel(x, c, ctx, c_ctx, norm1_g, norm2_g, ada_w, ada_b, w_in, w_out, conv_w, conv_b, conv_ln_g, conv_ln_b,
           na_rpb, lru_conv_w, lru_conv_b, lru_wx, lru_bx, lru_wa, lru_ba, lru_lambda, mlp_w1, mlp_w2, final_g):
    depth = w_in.shape[0]
    bsz, _, d = x.shape
    conv_dim = conv_w.shape[-1]
    lru_dim = lru_conv_w.shape[-1]
    na_dim = NA_HEADS * NA_HEAD_DIM
    dims = dict(conv_dim=conv_dim, na_dim=na_dim, lru_dim=lru_dim)

    cond = jnp.concatenate([c, c_ctx[None, :]], axis=0)
    cond = jnp.pad(cond, ((0, -(bsz + 1) % SUBLANES), (0, 0)))
    mod_all = _modulation(cond, ada_w, ada_b)
    cx = ctx
    for l in range(depth):
        last = l == depth - 1
        mod = mod_all[l, :bsz].reshape(bsz, 6, d)
        mod_c = mod_all[l, bsz:bsz + 1].reshape(1, 6, d)
        w_in_l = w_in[l].astype(BF16)
        g1 = norm1_g[l].reshape(1, d)

        u, q, k, v, rx, rg = _in_proj(x, mod, g1, w_in_l, **dims)
        cu, cq, ck, cv, crx, crg = _in_proj(cx, mod_c, g1, w_in_l, **dims)

        y_a = _conformer(u, conv_w[l], conv_b[l], conv_ln_g[l], conv_ln_b[l])
        y_b = _na_latent(q, k, v, ck, cv, na_rpb[l])
        y_c, yc_c = _rglru(rx, rg, crx, crg, lru_conv_w[l], lru_conv_b[l], lru_wx[l], lru_bx[l],
                           lru_wa[l], lru_ba[l], lru_lambda[l])

        w_out_l = w_out[l].astype(BF16)
        w1_l = mlp_w1[l].astype(BF16)
        w2_l = mlp_w2[l].astype(BF16)
        g2 = norm2_g[l].reshape(1, d)
        gf = final_g.reshape(1, d)
        x = _out_mlp(x, y_a, y_b, y_c, mod, g2, gf, w_out_l, w1_l, w2_l, final_norm=last)
        if not last:
            yc_a = _conformer(cu, conv_w[l], conv_b[l], conv_ln_g[l], conv_ln_b[l])
            yc_b = _na_context(cq, ck, cv)
            cx = _out_mlp(cx, yc_a, yc_b, yc_c, mod_c, g2, gf, w_out_l, w1_l, w2_l, final_norm=False)
    return x
```

```python
import functools

import jax
import jax.numpy as jnp
from jax import lax
from jax.experimental import pallas as pl
from jax.experimental.pallas import tpu as pltpu

GRID_W = 64
CONV_WIDTH = 31
NA_HEADS = 8
NA_HEAD_DIM = 64
NA_ROWS = 8
NA_COLS = 16
LRU_BLOCKS = 4
LRU_CONV = 4
LRU_C = 8.0
EPS = 1e-6
NEG_INF = -1e30

SUBLANES = 8
LANES = 128
HEADS_PER_STEP = 4
HEAD_LANES = HEADS_PER_STEP * NA_HEAD_DIM
VMEM_LIMIT_BYTES = 56 * 1024 * 1024

F32 = jnp.float32
BF16 = jnp.bfloat16


def _sigmoid(x):
    return 1.0 / (1.0 + jnp.exp(-x))


def _gelu_tanh(x):
    return 0.5 * x * (1.0 + jnp.tanh(0.7978845608028654 * (x + 0.044715 * (x * x * x))))


def _rms(x):
    return x * lax.rsqrt(jnp.mean(x * x, axis=-1, keepdims=True) + EPS)


def _params(*sem):
    return pltpu.CompilerParams(dimension_semantics=sem, vmem_limit_bytes=VMEM_LIMIT_BYTES)


def _mod_kernel(c_ref, w_ref, b_ref, o_ref):
    c = c_ref[...]
    s = c * _sigmoid(c)
    o_ref[0] = jnp.dot(s, w_ref[0], preferred_element_type=F32,
                       precision=lax.Precision.HIGHEST) + b_ref[0]


def _modulation(cc, ada_w, ada_b):
    depth, d, n = ada_w.shape
    r = cc.shape[0]
    tn = n // 4
    return pl.pallas_call(
        _mod_kernel,
        grid=(depth, n // tn),
        in_specs=[pl.BlockSpec((r, d), lambda l, j: (0, 0)),
                  pl.BlockSpec((1, d, tn), lambda l, j: (l, 0, j)),
                  pl.BlockSpec((1, 1, tn), lambda l, j: (l, 0, j))],
        out_specs=pl.BlockSpec((1, r, tn), lambda l, j: (l, 0, j)),
        out_shape=jax.ShapeDtypeStruct((depth, r, n), F32),
        compiler_params=_params("parallel", "parallel"),
        name="adaln_modulation",
    )(cc, ada_w, ada_b.reshape(depth, 1, n))


def _in_proj_kernel(x_ref, mod_ref, g_ref, w_ref, u_ref, q_ref, k_ref, v_ref, rx_ref, rg_ref, *,
                    conv_dim, na_dim, lru_dim):
    x = x_ref[0]
    h = _rms(x) * g_ref[...]
    h = h * (1.0 + mod_ref[0, 1:2, :]) + mod_ref[0, 0:1, :]
    hb = h.astype(BF16)

    def proj(off, width):
        return jnp.dot(hb, w_ref[:, off:off + width], preferred_element_type=F32)

    off = 0
    val = proj(off, conv_dim); off += conv_dim
    gate = proj(off, conv_dim); off += conv_dim
    u_ref[0] = val * _sigmoid(gate)
    q_ref[0] = (proj(off, na_dim) * (NA_HEAD_DIM ** -0.5)).astype(BF16); off += na_dim
    k_ref[0] = proj(off, na_dim).astype(BF16); off += na_dim
    v_ref[0] = proj(off, na_dim).astype(BF16); off += na_dim
    rx_ref[0] = proj(off, lru_dim); off += lru_dim
    rg_ref[0] = proj(off, lru_dim)


IN_PROJ_TM = 1024


def _in_proj(x, mod, g, w_bf16, *, conv_dim, na_dim, lru_dim):
    b, n, d = x.shape
    tm = min(n, IN_PROJ_TM)
    per_batch = mod.shape[0] != 1
    tok = lambda width, dt: (pl.BlockSpec((1, tm, width), lambda i, j: (i, j, 0)),
                             jax.ShapeDtypeStruct((b, n, width), dt))
    outs = [tok(conv_dim, F32), tok(na_dim, BF16), tok(na_dim, BF16), tok(na_dim, BF16),
            tok(lru_dim, F32), tok(lru_dim, F32)]
    return pl.pallas_call(
        functools.partial(_in_proj_kernel, conv_dim=conv_dim, na_dim=na_dim, lru_dim=lru_dim),
        grid=(b, n // tm),
        in_specs=[pl.BlockSpec((1, tm, d), lambda i, j: (i, j, 0)),
                  pl.BlockSpec((1, 6, d), (lambda i, j: (i, 0, 0)) if per_batch else (lambda i, j: (0, 0, 0))),
                  pl.BlockSpec((1, d), lambda i, j: (0, 0)),
                  pl.BlockSpec(w_bf16.shape, lambda i, j: (0, 0))],
        out_specs=[o[0] for o in outs],
        out_shape=[o[1] for o in outs],
        compiler_params=_params("parallel", "parallel"),
        name="in_proj",
    )(x, mod, g, w_bf16)


CONV_TB = 32
CONV_PAD = 16
CONV_NORM_TB = 128


def _shifted_tiles(tiles, off, count, sub):
    a, r = divmod(off, SUBLANES)
    if r == 0:
        return tiles[a:a + count]
    rolled = [pltpu.roll(t, SUBLANES - r, 0) for t in tiles[a:a + count + 1]]
    keep = sub < SUBLANES - r
    return [jnp.where(keep, rolled[i], rolled[i + 1]) for i in range(count)]


def _conformer_kernel(u_ref, w_ref, b_ref, lg_ref, lb_ref, o_ref, pad_ref, w8_ref, conv_ref, *, n):
    c = u_ref.shape[-1]
    zeros = jnp.zeros((CONV_PAD, c), F32)
    pad_ref[0:CONV_PAD, :] = zeros
    pad_ref[n + CONV_PAD:n + 2 * CONV_PAD, :] = zeros
    pad_ref[CONV_PAD:n + CONV_PAD, :] = u_ref[0]
    for k in range(CONV_WIDTH):
        w8_ref[k] = jnp.broadcast_to(w_ref[k:k + 1, :], (SUBLANES, c))
    sub = lax.broadcasted_iota(jnp.int32, (SUBLANES, c), 0)
    bias = jnp.broadcast_to(b_ref[...], (SUBLANES, c))
    half = CONV_WIDTH // 2
    out_tiles = CONV_TB // SUBLANES
    win_tiles = (CONV_TB + 2 * CONV_PAD) // SUBLANES

    def block(i, carry):
        start = pl.multiple_of(i * CONV_TB, CONV_TB)
        tiles = [pad_ref[pl.ds(start + SUBLANES * m, SUBLANES), :] for m in range(win_tiles)]
        acc = [bias] * out_tiles
        for r in range(SUBLANES):
            sh = _shifted_tiles(tiles, r, win_tiles - 1, sub)
            for a in range(win_tiles - out_tiles):
                k = r + SUBLANES * a - (CONV_PAD - half)
                if 0 <= k < CONV_WIDTH:
                    w = w8_ref[k]
                    acc = [acc[j] + w * sh[j + a] for j in range(out_tiles)]
        conv_ref[pl.ds(start, CONV_TB), :] = jnp.concatenate(acc, axis=0)
        return carry

    lax.fori_loop(0, n // CONV_TB, block, 0)

    def norm_block(i, carry):
        start = pl.multiple_of(i * CONV_NORM_TB, CONV_NORM_TB)
        conv = conv_ref[pl.ds(start, CONV_NORM_TB), :]
        mu = jnp.mean(conv, axis=-1, keepdims=True)
        cen = conv - mu
        var = jnp.mean(cen * cen, axis=-1, keepdims=True)
        y = cen * lax.rsqrt(var + EPS) * lg_ref[...] + lb_ref[...]
        o_ref[0, pl.ds(start, CONV_NORM_TB), :] = (y * _sigmoid(y)).astype(o_ref.dtype)
        return carry

    lax.fori_loop(0, n // CONV_NORM_TB, norm_block, 0, unroll=4)


def _conformer(u, w, b, ln_g, ln_b):
    bsz, n, c = u.shape
    row = pl.BlockSpec((1, c), lambda i: (0, 0))
    return pl.pallas_call(
        functools.partial(_conformer_kernel, n=n),
        grid=(bsz,),
        in_specs=[pl.BlockSpec((1, n, c), lambda i: (i, 0, 0)),
                  pl.BlockSpec(w.shape, lambda i: (0, 0)), row, row, row],
        out_specs=pl.BlockSpec((1, n, c), lambda i: (i, 0, 0)),
        out_shape=jax.ShapeDtypeStruct((bsz, n, c), BF16),
        scratch_shapes=[pltpu.VMEM((n + 2 * CONV_PAD, c), F32), pltpu.VMEM((CONV_WIDTH, SUBLANES, c), F32),
                        pltpu.VMEM((n, c), F32)],
        compiler_params=_params("parallel"),
        name="conformer_conv",
    )(u, w, b.reshape(1, c), ln_g.reshape(1, c), ln_b.reshape(1, c))


def _head_of_lane():
    return lax.broadcasted_iota(jnp.int32, (1, HEAD_LANES), 1) // NA_HEAD_DIM


def _stack_heads(qr, head):
    zero = jnp.zeros_like(qr)
    return jnp.concatenate([jnp.where(head == h, qr, zero) for h in range(HEADS_PER_STEP)], axis=0)


def _unstack_heads(o, head, m):
    out = o[0:m]
    for h in range(1, HEADS_PER_STEP):
        out = jnp.where(head == h, o[h * m:(h + 1) * m], out)
    return out


NA_PIPE_UNROLL = 2


def _build_na_bias(rpb_ref, bias_ref, group):
    qi = lax.broadcasted_iota(jnp.int32, (GRID_W, GRID_W), 0)
    ki = lax.broadcasted_iota(jnp.int32, (GRID_W, GRID_W), 1)
    rel = ki - qi
    col_start = jnp.clip(qi - NA_COLS // 2, 0, GRID_W - NA_COLS)
    valid = (ki >= col_start) & (ki < col_start + NA_COLS)
    n_r = 2 * NA_ROWS - 1
    n_c = 2 * NA_COLS - 1

    def per_head(h, carry):
        q0 = pl.multiple_of(h * GRID_W, GRID_W)
        for ri in range(n_r):
            base = ((group * HEADS_PER_STEP + h) * n_r + ri) * n_c
            t = jnp.full((GRID_W, GRID_W), NEG_INF, F32)
            for dc in range(n_c):
                t = jnp.where(rel == dc - (NA_COLS - 1), rpb_ref[base + dc], t)
            t = jnp.where(valid, t, NEG_INF)
            for d in range(NA_ROWS):
                j = ri + d - (NA_ROWS - 1)
                if 0 <= j < NA_ROWS:
                    bias_ref[d, pl.ds(q0, GRID_W), j * GRID_W:(j + 1) * GRID_W] = t
        return carry

    lax.fori_loop(0, HEADS_PER_STEP, per_head, 0)


def _lane_reduce(tiles, combine, reduce):
    by_width = {}
    for t in tiles:
        w = t.shape[-1]
        by_width[w] = t if w not in by_width else combine(by_width[w], t)
    parts = [reduce(t, axis=-1, keepdims=True) for t in by_width.values()]
    out = parts[0]
    for part in parts[1:]:
        out = combine(out, part)
    return out


def _na_latent_kernel(rpb_ref, q_ref, k_ref, v_ref, kc_ref, vc_ref, o_ref, bias_ref,
                      s0_ref, s1_ref, p0_ref, p1_ref, inv0_ref, inv1_ref, *, rows):
    @pl.when(pl.program_id(1) == 0)
    def _():
        _build_na_bias(rpb_ref, bias_ref, pl.program_id(0))

    head = _head_of_lane()
    band = NA_ROWS * GRID_W
    n_keys = s0_ref.shape[1]
    nt = (((1,), (1,)), ((), ()))
    s_refs = (s0_ref, s1_ref)
    p_refs = (p0_ref, p1_ref)
    inv_refs = (inv0_ref, inv1_ref)
    bounds = list(range(0, band, HEAD_LANES)) + [band, n_keys]
    spans = list(zip(bounds[:-1], bounds[1:]))

    def band_start(r):
        return jnp.clip(r - NA_ROWS // 2, 0, rows - NA_ROWS)

    def scores(r, slot):
        rs = band_start(r)
        qs = _stack_heads(q_ref[0, pl.ds(pl.multiple_of(r * GRID_W, GRID_W), GRID_W), :], head)
        kb = k_ref[0, pl.ds(pl.multiple_of(rs * GRID_W, GRID_W), band), :]
        s_refs[slot][:, 0:band] = lax.dot_general(qs, kb, nt, preferred_element_type=F32) + bias_ref[r - rs]
        s_refs[slot][:, band:] = lax.dot_general(qs, kc_ref[0], nt, preferred_element_type=F32)

    def softmax(slot):
        tiles = [s_refs[slot][:, lo:hi] for lo, hi in spans]
        m = _lane_reduce(tiles, jnp.maximum, jnp.max)
        probs = [jnp.exp(t - m) for t in tiles]
        for (lo, hi), p in zip(spans, probs):
            p_refs[slot][:, lo:hi] = p.astype(BF16)
        inv_refs[slot][...] = 1.0 / _lane_reduce(probs, jnp.add, jnp.sum)

    def values(r, slot):
        rs = band_start(r)
        vb = v_ref[0, pl.ds(pl.multiple_of(rs * GRID_W, GRID_W), band), :]
        o = jnp.dot(p_refs[slot][:, 0:band], vb, preferred_element_type=F32)
        o = o + jnp.dot(p_refs[slot][:, band:], vc_ref[0], preferred_element_type=F32)
        o = o * inv_refs[slot][...]
        q0 = pl.multiple_of(r * GRID_W, GRID_W)
        o_ref[0, pl.ds(q0, GRID_W), :] = _unstack_heads(o, head, GRID_W).astype(o_ref.dtype)

    def step(i, parity, do_scores=True, do_values=True, do_softmax=True):
        if do_scores:
            scores(i, parity)
        if do_values:
            values(i - 2, parity)
        if do_softmax:
            softmax(1 - parity)

    step(0, 0, do_values=False, do_softmax=False)
    step(1, 1, do_values=False)

    def steady(j, carry):
        base = NA_PIPE_UNROLL * (j + 1)
        for t in range(NA_PIPE_UNROLL):
            step(base + t, t % 2)
        return carry

    lax.fori_loop(0, rows // NA_PIPE_UNROLL - 1, steady, 0)
    step(rows, 0, do_scores=False)
    step(rows + 1, 1, do_scores=False, do_softmax=False)


NA_ROW_UNROLL = 8


def _na_latent_rows_kernel(rpb_ref, q_ref, k_ref, v_ref, kc_ref, vc_ref, o_ref, bias_ref, *, rows):
    @pl.when(pl.program_id(1) == 0)
    def _():
        _build_na_bias(rpb_ref, bias_ref, pl.program_id(0))

    head = _head_of_lane()
    band = NA_ROWS * GRID_W
    nt = (((1,), (1,)), ((), ()))
    bounds = list(range(0, band, HEAD_LANES)) + [band]

    def row_step(r, carry):
        rs = jnp.clip(r - NA_ROWS // 2, 0, rows - NA_ROWS)
        q0 = pl.multiple_of(r * GRID_W, GRID_W)
        k0 = pl.multiple_of(rs * GRID_W, GRID_W)
        qs = _stack_heads(q_ref[0, pl.ds(q0, GRID_W), :], head)
        s_loc = lax.dot_general(qs, k_ref[0, pl.ds(k0, band), :], nt, preferred_element_type=F32) + bias_ref[r - rs]
        s_ctx = lax.dot_general(qs, kc_ref[0], nt, preferred_element_type=F32)
        tiles = [s_loc[:, lo:hi] for lo, hi in zip(bounds[:-1], bounds[1:])] + [s_ctx]
        m = _lane_reduce(tiles, jnp.maximum, jnp.max)
        probs = [jnp.exp(t - m) for t in tiles]
        inv = 1.0 / _lane_reduce(probs, jnp.add, jnp.sum)
        o = jnp.dot(jnp.concatenate(probs[:-1], axis=-1).astype(BF16), v_ref[0, pl.ds(k0, band), :],
                    preferred_element_type=F32)
        o = (o + jnp.dot(probs[-1].astype(BF16), vc_ref[0], preferred_element_type=F32)) * inv
        o_ref[0, pl.ds(q0, GRID_W), :] = _unstack_heads(o, head, GRID_W).astype(o_ref.dtype)
        return carry

    lax.fori_loop(0, rows, row_step, 0, unroll=NA_ROW_UNROLL)


def _na_latent(q, k, v, kc, vc, rpb, *, pipelined):
    bsz, n, na_dim = q.shape
    nc = kc.shape[1]
    groups = na_dim // HEAD_LANES
    rows = n // GRID_W
    assert rows >= NA_ROWS and rows % NA_PIPE_UNROLL == 0 and rows % NA_ROW_UNROLL == 0
    stacked = HEADS_PER_STEP * GRID_W
    n_keys = NA_ROWS * GRID_W + nc
    lat = pl.BlockSpec((1, n, HEAD_LANES), lambda g, i: (i, 0, g))
    ctx = pl.BlockSpec((1, nc, HEAD_LANES), lambda g, i: (i, 0, g))
    scratch = [pltpu.VMEM((NA_ROWS, stacked, NA_ROWS * GRID_W), F32)]
    if pipelined:
        scratch += ([pltpu.VMEM((stacked, n_keys), F32)] * 2
                    + [pltpu.VMEM((stacked, n_keys), BF16)] * 2
                    + [pltpu.VMEM((stacked, 1), F32)] * 2)
    return pl.pallas_call(
        functools.partial(_na_latent_kernel if pipelined else _na_latent_rows_kernel, rows=rows),
        grid=(groups, bsz),
        in_specs=[pl.BlockSpec(memory_space=pltpu.SMEM), lat, lat, lat, ctx, ctx],
        out_specs=lat,
        out_shape=jax.ShapeDtypeStruct((bsz, n, na_dim), BF16),
        scratch_shapes=scratch,
        compiler_params=_params("arbitrary", "arbitrary"),
        name="na_latent_pipelined" if pipelined else "na_latent_rows",
    )(rpb.reshape(-1), q, k, v, kc, vc)


def _na_context_kernel(q_ref, k_ref, v_ref, o_ref):
    head = _head_of_lane()
    n = q_ref.shape[1]
    qs = _stack_heads(q_ref[0], head)
    s = lax.dot_general(qs, k_ref[0], (((1,), (1,)), ((), ())), preferred_element_type=F32)
    p = jnp.exp(s - jnp.max(s, axis=-1, keepdims=True))
    denom = jnp.sum(p, axis=-1, keepdims=True)
    o = jnp.dot(p.astype(BF16), v_ref[0], preferred_element_type=F32) * (1.0 / denom)
    o_ref[0] = _unstack_heads(o, head, n).astype(o_ref.dtype)


def _na_context(q, k, v):
    bsz, n, na_dim = q.shape
    blk = pl.BlockSpec((1, n, HEAD_LANES), lambda i, g: (i, 0, g))
    return pl.pallas_call(
        _na_context_kernel,
        grid=(bsz, na_dim // HEAD_LANES),
        in_specs=[blk, blk, blk],
        out_specs=blk,
        out_shape=jax.ShapeDtypeStruct((bsz, n, na_dim), BF16),
        compiler_params=_params("parallel", "parallel"),
        name="na_context",
    )(q, k, v)


LRU_TB = 256
LRU_PAD = 8


def _rglru_kernel(xl_ref, gl_ref, xc_ref, gc_ref, cw_ref, vec_ref, wx_ref, wa_ref, yl_ref, yc_ref,
                  padl_ref, padc_ref, a_ref, b_ref, h_ref, *, n_lat, n_ctx):
    c = xl_ref.shape[-1]
    zeros = jnp.zeros((LRU_PAD, c), F32)
    for pad_ref, src_ref, n in ((padc_ref, xc_ref, n_ctx), (padl_ref, xl_ref, n_lat)):
        pad_ref[0:LRU_PAD, :] = zeros
        pad_ref[n + LRU_PAD:n + 2 * LRU_PAD, :] = zeros
        pad_ref[LRU_PAD:n + LRU_PAD, :] = src_ref[0]
    sub = lax.broadcasted_iota(jnp.int32, (SUBLANES, c), 0)

    for direction in range(2):
        reverse = direction == 1
        conv_b = vec_ref[direction, 0:1, :]
        bx = vec_ref[direction, 1:2, :]
        ba = vec_ref[direction, 2:3, :]
        neg_lam = -vec_ref[direction, 3:4, :]
        softplus = jnp.maximum(neg_lam, 0.0) + jnp.log(1.0 + jnp.exp(-jnp.abs(neg_lam)))
        decay = -LRU_C * softplus

        conv_b8 = jnp.broadcast_to(conv_b, (SUBLANES, c))
        cw8 = [jnp.broadcast_to(cw_ref[direction, k:k + 1, :], (SUBLANES, c)) for k in range(LRU_CONV)]

        def coeffs(pad_ref, base, n):
            tb = min(n, LRU_TB)
            nt = tb // SUBLANES

            def block(i, carry):
                start = pl.multiple_of(i * tb, tb)
                tiles = [pad_ref[pl.ds(start + SUBLANES * m, SUBLANES), :]
                         for m in range(nt + 2 * LRU_PAD // SUBLANES)]
                u_tiles = [conv_b8] * nt
                for k in range(LRU_CONV):
                    off = LRU_PAD + k - (0 if reverse else LRU_CONV - 1)
                    sh = _shifted_tiles(tiles, off, nt, sub)
                    u_tiles = [u_tiles[j] + cw8[k] * sh[j] for j in range(nt)]
                u = jnp.concatenate(u_tiles, axis=0)
                ub = u.astype(BF16)
                gx = _sigmoid(jnp.dot(ub, wx_ref[direction], preferred_element_type=F32) + bx)
                ga = _sigmoid(jnp.dot(ub, wa_ref[direction], preferred_element_type=F32) + ba)
                a = jnp.exp(decay * ga)
                rem = 1.0 - a * a
                coef = jnp.where(rem > 0.0, rem * lax.rsqrt(rem), 0.0)
                a_ref[pl.ds(base + start, tb), :] = a
                b_ref[pl.ds(base + start, tb), :] = coef * (gx * u)
                return carry

            lax.fori_loop(0, n // tb, block, 0)

        coeffs(padc_ref, 0, n_ctx)
        coeffs(padl_ref, n_ctx, n_lat)

        def scan(base, n, hb):
            chunks = n // SUBLANES

            def chunk(j, hb):
                jj = chunks - 1 - j if reverse else j
                rows = pl.ds(pl.multiple_of(base + jj * SUBLANES, SUBLANES), SUBLANES)
                a = a_ref[rows, :]
                b = b_ref[rows, :]
                for d in (1, 2, 4):
                    shift = SUBLANES - d if reverse else d
                    keep = (sub < SUBLANES - d) if reverse else (sub >= d)
                    ra = pltpu.roll(a, shift, 0)
                    rb = pltpu.roll(b, shift, 0)
                    b = jnp.where(keep, a * rb + b, b)
                    a = jnp.where(keep, a * ra, a)
                h = a * hb + b
                if reverse:
                    h_ref[rows, :] = h_ref[rows, :] + h
                else:
                    h_ref[rows, :] = h
                last = 0 if reverse else SUBLANES - 1
                a_last = jnp.broadcast_to(a[last:last + 1, :], (SUBLANES, c))
                b_last = jnp.broadcast_to(b[last:last + 1, :], (SUBLANES, c))
                return a_last * hb + b_last

            return lax.fori_loop(0, chunks, chunk, hb, unroll=8)

        hb = scan(0, n_ctx, jnp.zeros((SUBLANES, c), F32))
        scan(n_ctx, n_lat, hb)

    def gate_out(g_ref, y_ref, base, n):
        tb = min(n, LRU_TB)

        def block(i, carry):
            start = pl.multiple_of(i * tb, tb)
            g = g_ref[0, pl.ds(start, tb), :]
            y_ref[0, pl.ds(start, tb), :] = (_gelu_tanh(g) * h_ref[pl.ds(base + start, tb), :]).astype(y_ref.dtype)
            return carry

        lax.fori_loop(0, n // tb, block, 0)

    gate_out(gc_ref, yc_ref, 0, n_ctx)
    gate_out(gl_ref, yl_ref, n_ctx, n_lat)


def _block_diag(w):
    two, nb, m, _ = w.shape
    eye = jnp.eye(nb, dtype=w.dtype)
    return (w[:, :, :, None, :] * eye[None, :, None, :, None]).reshape(two, nb * m, nb * m)


def _rglru(xl, gl, xc, gc, conv_w, conv_b, wx, bx, wa, ba, lam):
    bsz, n_lat, c = xl.shape
    n_ctx = xc.shape[1]
    vec = jnp.stack([conv_b, bx, ba, lam], axis=1)
    wxd = _block_diag(wx).astype(BF16)
    wad = _block_diag(wa).astype(BF16)
    lat = pl.BlockSpec((1, n_lat, c), lambda i: (i, 0, 0))
    ctx = pl.BlockSpec((1, n_ctx, c), lambda i: (i, 0, 0))
    full = lambda a: pl.BlockSpec(a.shape, lambda i: (0,) * a.ndim)
    return pl.pallas_call(
        functools.partial(_rglru_kernel, n_lat=n_lat, n_ctx=n_ctx),
        grid=(bsz,),
        in_specs=[lat, lat, ctx, ctx, full(conv_w), full(vec), full(wxd), full(wad)],
        out_specs=[lat, ctx],
        out_shape=[jax.ShapeDtypeStruct((bsz, n_lat, c), BF16), jax.ShapeDtypeStruct((bsz, n_ctx, c), BF16)],
        scratch_shapes=[pltpu.VMEM((n_lat + 2 * LRU_PAD, c), F32), pltpu.VMEM((n_ctx + 2 * LRU_PAD, c), F32),
                        pltpu.VMEM((n_ctx + n_lat, c), F32), pltpu.VMEM((n_ctx + n_lat, c), F32),
                        pltpu.VMEM((n_ctx + n_lat, c), F32)],
        compiler_params=_params("parallel"),
        name="rglru",
    )(xl, gl, xc, gc, conv_w, vec, wxd, wad)


FF_CHUNK = 1024


def _out_mlp_body(x, ya, yb, yc, mod_ref, g2_ref, gf_ref, wo_ref, w1_ref, w2_ref, final_norm):
    ca, cb, cc = ya.shape[-1], yb.shape[-1], yc.shape[-1]
    mix = (jnp.dot(ya, wo_ref[0:ca, :], preferred_element_type=F32)
           + jnp.dot(yb, wo_ref[ca:ca + cb, :], preferred_element_type=F32)
           + jnp.dot(yc, wo_ref[ca + cb:ca + cb + cc, :], preferred_element_type=F32))
    x1 = x + mod_ref[0, 2:3, :] * mix
    h2 = _rms(x1) * g2_ref[...]
    h2 = (h2 * (1.0 + mod_ref[0, 4:5, :]) + mod_ref[0, 3:4, :]).astype(BF16)
    acc = jnp.zeros(x1.shape, F32)
    for j in range(w1_ref.shape[1] // FF_CHUNK):
        cols = slice(j * FF_CHUNK, (j + 1) * FF_CHUNK)
        hid = jnp.maximum(jnp.dot(h2, w1_ref[:, cols], preferred_element_type=F32), 0.0)
        acc = acc + jnp.dot((hid * hid).astype(BF16), w2_ref[cols, :], preferred_element_type=F32)
    out = x1 + mod_ref[0, 5:6, :] * acc
    if final_norm:
        out = _rms(out) * gf_ref[...]
    return out


def _out_mlp_kernel(x_ref, ya_ref, yb_ref, yc_ref, mod_ref, g2_ref, gf_ref, wo_ref, w1_ref, w2_ref, o_ref, *,
                    final_norm):
    o_ref[0] = _out_mlp_body(x_ref[0], ya_ref[0], yb_ref[0], yc_ref[0], mod_ref, g2_ref, gf_ref,
                             wo_ref, w1_ref, w2_ref, final_norm)


def _conformer_tile(u_ref, w8_ref, bias, lg_ref, lb_ref, ya_ref, slot, tile, *, n, tm, sub):
    half = CONV_WIDTH // 2
    out_tiles = CONV_TB // SUBLANES
    win_tiles = (CONV_TB + 2 * CONV_PAD) // SUBLANES
    pad_tiles = CONV_PAD // SUBLANES
    blocks = tm // CONV_TB
    first = tile == 0
    last = tile == n // tm - 1
    zero = jnp.zeros((SUBLANES, u_ref.shape[-1]), F32)
    done = []
    for blk in range(blocks):
        base = tile * tm + (CONV_TB * blk - CONV_PAD)
        tiles = []
        for m in range(win_tiles):
            start = base + SUBLANES * m
            if blk == 0 and m < pad_tiles:
                t = u_ref[0, pl.ds(pl.multiple_of(jnp.maximum(start, 0), SUBLANES), SUBLANES), :]
                t = jnp.where(first, zero, t)
            elif blk == blocks - 1 and m >= win_tiles - pad_tiles:
                t = u_ref[0, pl.ds(pl.multiple_of(jnp.minimum(start, n - SUBLANES), SUBLANES), SUBLANES), :]
                t = jnp.where(last, zero, t)
            else:
                t = u_ref[0, pl.ds(pl.multiple_of(start, SUBLANES), SUBLANES), :]
            tiles.append(t)
        acc = [bias] * out_tiles
        for r in range(SUBLANES):
            sh = _shifted_tiles(tiles, r, win_tiles - 1, sub)
            for a in range(win_tiles - out_tiles):
                k = r + SUBLANES * a - (CONV_PAD - half)
                if 0 <= k < CONV_WIDTH:
                    w = w8_ref[k]
                    acc = [acc[j] + w * sh[j + a] for j in range(out_tiles)]
        done.extend(acc)
        if len(done) == CONV_NORM_TB // SUBLANES:
            conv = jnp.concatenate(done, axis=0)
            done = []
            mu = jnp.mean(conv, axis=-1, keepdims=True)
            cen = conv - mu
            var = jnp.mean(cen * cen, axis=-1, keepdims=True)
            y = cen * lax.rsqrt(var + EPS) * lg_ref[...] + lb_ref[...]
            row0 = CONV_TB * (blk + 1) - CONV_NORM_TB
            ya_ref[slot, row0:row0 + CONV_NORM_TB, :] = (y * _sigmoid(y)).astype(ya_ref.dtype)


def _conv_out_mlp_kernel(x_ref, u_ref, yb_ref, yc_ref, cw_ref, cb_ref, lg_ref, lb_ref, mod_ref, g2_ref, gf_ref,
                         wo_ref, w1_ref, w2_ref, o_ref, ya_ref, w8_ref, *, final_norm, n, tm):
    c = u_ref.shape[-1]
    tiles_per_seq = n // tm
    step = pl.program_id(0) * tiles_per_seq + pl.program_id(1)
    cur = step % 2
    for k in range(CONV_WIDTH):
        w8_ref[k] = jnp.broadcast_to(cw_ref[k:k + 1, :], (SUBLANES, c))
    sub = lax.broadcasted_iota(jnp.int32, (SUBLANES, c), 0)
    bias = jnp.broadcast_to(cb_ref[...], (SUBLANES, c))
    conv = functools.partial(_conformer_tile, u_ref, w8_ref, bias, lg_ref, lb_ref, ya_ref, n=n, tm=tm, sub=sub)

    @pl.when(step == 0)
    def _():
        conv(0, 0)

    o_ref[0] = _out_mlp_body(x_ref[0], ya_ref[cur], yb_ref[0], yc_ref[0], mod_ref, g2_ref, gf_ref,
                             wo_ref, w1_ref, w2_ref, final_norm)
    conv(1 - cur, (pl.program_id(1) + 1) % tiles_per_seq)


def _out_mlp(x, ya, yb, yc, mod, g2, gf, wo, w1, w2, *, final_norm):
    b, n, d = x.shape
    tm = min(n, 512)
    per_batch = mod.shape[0] != 1
    tok = lambda a: pl.BlockSpec((1, tm, a.shape[-1]), lambda i, j: (i, j, 0))
    resident = lambda a: pl.BlockSpec(a.shape, lambda i, j: (0, 0), pipeline_mode=pl.Buffered(1))
    return pl.pallas_call(
        functools.partial(_out_mlp_kernel, final_norm=final_norm),
        grid=(b, n // tm),
        in_specs=[tok(x), tok(ya), tok(yb), tok(yc),
                  pl.BlockSpec((1, 6, d), (lambda i, j: (i, 0, 0)) if per_batch else (lambda i, j: (0, 0, 0))),
                  pl.BlockSpec((1, d), lambda i, j: (0, 0)), pl.BlockSpec((1, d), lambda i, j: (0, 0)),
                  resident(wo), resident(w1), resident(w2)],
        out_specs=tok(x),
        out_shape=jax.ShapeDtypeStruct((b, n, d), F32),
        compiler_params=_params("parallel", "parallel"),
        name="out_proj_mlp",
    )(x, ya, yb, yc, mod, g2, gf, wo, w1, w2)


def _conv_out_mlp(x, u, yb, yc, conv_w, conv_b, ln_g, ln_b, mod, g2, gf, wo, w1, w2, *, final_norm):
    b, n, d = x.shape
    c = u.shape[-1]
    tm = min(n, 512)
    nt = n // tm
    assert tm % CONV_NORM_TB == 0 and CONV_NORM_TB % CONV_TB == 0
    tok = lambda a: pl.BlockSpec((1, tm, a.shape[-1]), lambda i, j: (i, j, 0))
    row = lambda width: pl.BlockSpec((1, width), lambda i, j: (0, 0))
    resident = lambda a: pl.BlockSpec(a.shape, lambda i, j: (0, 0), pipeline_mode=pl.Buffered(1))
    next_batch = lambda i, j: (jnp.minimum(i + (j + 1) // nt, b - 1), 0, 0)
    return pl.pallas_call(
        functools.partial(_conv_out_mlp_kernel, final_norm=final_norm, n=n, tm=tm),
        grid=(b, nt),
        in_specs=[tok(x), pl.BlockSpec((1, n, c), next_batch), tok(yb), tok(yc),
                  pl.BlockSpec(conv_w.shape, lambda i, j: (0, 0)), row(c), row(c), row(c),
                  pl.BlockSpec((1, 6, d), lambda i, j: (i, 0, 0)), row(d), row(d),
                  resident(wo), resident(w1), resident(w2)],
        out_specs=tok(x),
        out_shape=jax.ShapeDtypeStruct((b, n, d), F32),
        scratch_shapes=[pltpu.VMEM((2, tm, c), BF16), pltpu.VMEM((CONV_WIDTH, SUBLANES, c), F32)],
        compiler_params=_params("arbitrary", "arbitrary"),
        name="conv_out_proj_mlp",
    )(x, u, yb, yc, conv_w, conv_b.reshape(1, c), ln_g.reshape(1, c), ln_b.reshape(1, c), mod, g2, gf, wo, w1, w2)


def kernel(x, c, ctx, c_ctx, norm1_g, norm2_g, ada_w, ada_b, w_in, w_out, conv_w, conv_b, conv_ln_g, conv_ln_b,
           na_rpb, lru_conv_w, lru_conv_b, lru_wx, lru_bx, lru_wa, lru_ba, lru_lambda, mlp_w1, mlp_w2, final_g):
    depth = w_in.shape[0]
    bsz, _, d = x.shape
    conv_dim = conv_w.shape[-1]
    lru_dim = lru_conv_w.shape[-1]
    na_dim = NA_HEADS * NA_HEAD_DIM
    dims = dict(conv_dim=conv_dim, na_dim=na_dim, lru_dim=lru_dim)

    cond = jnp.concatenate([c, c_ctx[None, :]], axis=0)
    cond = jnp.pad(cond, ((0, -(bsz + 1) % SUBLANES), (0, 0)))
    mod_all = _modulation(cond, ada_w, ada_b)
    cx = ctx
    for l in range(depth):
        last = l == depth - 1
        mod = mod_all[l, :bsz].reshape(bsz, 6, d)
        mod_c = mod_all[l, bsz:bsz + 1].reshape(1, 6, d)
        w_in_l = w_in[l].astype(BF16)
        g1 = norm1_g[l].reshape(1, d)

        u, q, k, v, rx, rg = _in_proj(x, mod, g1, w_in_l, **dims)
        cu, cq, ck, cv, crx, crg = _in_proj(cx, mod_c, g1, w_in_l, **dims)

        variant_a = l % 2 == 0
        y_b = _na_latent(q, k, v, ck, cv, na_rpb[l], pipelined=variant_a)
        y_c, yc_c = _rglru(rx, rg, crx, crg, lru_conv_w[l], lru_conv_b[l], lru_wx[l], lru_bx[l],
                           lru_wa[l], lru_ba[l], lru_lambda[l])

        w_out_l = w_out[l].astype(BF16)
        w1_l = mlp_w1[l].astype(BF16)
        w2_l = mlp_w2[l].astype(BF16)
        g2 = norm2_g[l].reshape(1, d)
        gf = final_g.reshape(1, d)
        if variant_a:
            x = _conv_out_mlp(x, u, y_b, y_c, conv_w[l], conv_b[l], conv_ln_g[l], conv_ln_b[l], mod, g2, gf,
                              w_out_l, w1_l, w2_l, final_norm=last)
        else:
            y_a = _conformer(u, conv_w[l], conv_b[l], conv_ln_g[l], conv_ln_b[l])
            x = _out_mlp(x, y_a, y_b, y_c, mod, g2, gf, w_out_l, w1_l, w2_l, final_norm=last)
        if not last:
            yc_a = _conformer(cu, conv_w[l], conv_b[l], conv_ln_g[l], conv_ln_b[l])
            yc_b = _na_context(cq, ck, cv)
            cx = _out_mlp(cx, yc_a, yc_b, yc_c, mod_c, g2, gf, w_out_l, w1_l, w2_l, final_norm=False)
    return x
```

```python
import functools

import jax
import jax.numpy as jnp
from jax import lax
from jax.experimental import pallas as pl
from jax.experimental.pallas import tpu as pltpu

GRID_W = 64
CONV_WIDTH = 31
NA_HEADS = 8
NA_HEAD_DIM = 64
NA_ROWS = 8
NA_COLS = 16
LRU_BLOCKS = 4
LRU_CONV = 4
LRU_C = 8.0
EPS = 1e-6
NEG_INF = -1e30

SUBLANES = 8
LANES = 128
HEADS_PER_STEP = 4
HEAD_LANES = HEADS_PER_STEP * NA_HEAD_DIM
VMEM_LIMIT_BYTES = 56 * 1024 * 1024

F32 = jnp.float32
BF16 = jnp.bfloat16


def _sigmoid(x):
    return 1.0 / (1.0 + jnp.exp(-x))


def _gelu_tanh(x):
    return 0.5 * x * (1.0 + jnp.tanh(0.7978845608028654 * (x + 0.044715 * (x * x * x))))


def _rms(x):
    return x * lax.rsqrt(jnp.mean(x * x, axis=-1, keepdims=True) + EPS)


def _params(*sem):
    return pltpu.CompilerParams(dimension_semantics=sem, vmem_limit_bytes=VMEM_LIMIT_BYTES)


def _mod_kernel(c_ref, w_ref, b_ref, o_ref):
    c = c_ref[...]
    s = c * _sigmoid(c)
    o_ref[0] = jnp.dot(s, w_ref[0], preferred_element_type=F32,
                       precision=lax.Precision.HIGHEST) + b_ref[0]


def _modulation(cc, ada_w, ada_b):
    depth, d, n = ada_w.shape
    r = cc.shape[0]
    tn = n // 4
    return pl.pallas_call(
        _mod_kernel,
        grid=(depth, n // tn),
        in_specs=[pl.BlockSpec((r, d), lambda l, j: (0, 0)),
                  pl.BlockSpec((1, d, tn), lambda l, j: (l, 0, j)),
                  pl.BlockSpec((1, 1, tn), lambda l, j: (l, 0, j))],
        out_specs=pl.BlockSpec((1, r, tn), lambda l, j: (l, 0, j)),
        out_shape=jax.ShapeDtypeStruct((depth, r, n), F32),
        compiler_params=_params("parallel", "parallel"),
        name="adaln_modulation",
    )(cc, ada_w, ada_b.reshape(depth, 1, n))


def _in_proj_kernel(x_ref, mod_ref, g_ref, w_ref, u_ref, q_ref, k_ref, v_ref, rx_ref, rg_ref, *,
                    conv_dim, na_dim, lru_dim):
    x = x_ref[0]
    h = _rms(x) * g_ref[...]
    h = h * (1.0 + mod_ref[0, 1:2, :]) + mod_ref[0, 0:1, :]
    hb = h.astype(BF16)

    def proj(off, width):
        return jnp.dot(hb, w_ref[:, off:off + width], preferred_element_type=F32)

    off = 0
    val = proj(off, conv_dim); off += conv_dim
    gate = proj(off, conv_dim); off += conv_dim
    u_ref[0] = val * _sigmoid(gate)
    q_ref[0] = (proj(off, na_dim) * (NA_HEAD_DIM ** -0.5)).astype(BF16); off += na_dim
    k_ref[0] = proj(off, na_dim).astype(BF16); off += na_dim
    v_ref[0] = proj(off, na_dim).astype(BF16); off += na_dim
    rx_ref[0] = proj(off, lru_dim); off += lru_dim
    rg_ref[0] = proj(off, lru_dim)


IN_PROJ_TM = 1024


def _in_proj(x, mod, g, w_bf16, *, conv_dim, na_dim, lru_dim):
    b, n, d = x.shape
    tm = min(n, IN_PROJ_TM)
    per_batch = mod.shape[0] != 1
    tok = lambda width, dt: (pl.BlockSpec((1, tm, width), lambda i, j: (i, j, 0)),
                             jax.ShapeDtypeStruct((b, n, width), dt))
    outs = [tok(conv_dim, F32), tok(na_dim, BF16), tok(na_dim, BF16), tok(na_dim, BF16),
            tok(lru_dim, F32), tok(lru_dim, F32)]
    return pl.pallas_call(
        functools.partial(_in_proj_kernel, conv_dim=conv_dim, na_dim=na_dim, lru_dim=lru_dim),
        grid=(b, n // tm),
        in_specs=[pl.BlockSpec((1, tm, d), lambda i, j: (i, j, 0)),
                  pl.BlockSpec((1, 6, d), (lambda i, j: (i, 0, 0)) if per_batch else (lambda i, j: (0, 0, 0))),
                  pl.BlockSpec((1, d), lambda i, j: (0, 0)),
                  pl.BlockSpec(w_bf16.shape, lambda i, j: (0, 0))],
        out_specs=[o[0] for o in outs],
        out_shape=[o[1] for o in outs],
        compiler_params=_params("parallel", "parallel"),
        name="in_proj",
    )(x, mod, g, w_bf16)


CONV_TB = 32
CONV_PAD = 16
CONV_NORM_TB = 128


def _shifted_tiles(tiles, off, count, sub):
    a, r = divmod(off, SUBLANES)
    if r == 0:
        return tiles[a:a + count]
    rolled = [pltpu.roll(t, SUBLANES - r, 0) for t in tiles[a:a + count + 1]]
    keep = sub < SUBLANES - r
    return [jnp.where(keep, rolled[i], rolled[i + 1]) for i in range(count)]


def _conformer_kernel(u_ref, w_ref, b_ref, lg_ref, lb_ref, o_ref, pad_ref, w8_ref, conv_ref, *, n):
    c = u_ref.shape[-1]
    zeros = jnp.zeros((CONV_PAD, c), F32)
    pad_ref[0:CONV_PAD, :] = zeros
    pad_ref[n + CONV_PAD:n + 2 * CONV_PAD, :] = zeros
    pad_ref[CONV_PAD:n + CONV_PAD, :] = u_ref[0]
    for k in range(CONV_WIDTH):
        w8_ref[k] = jnp.broadcast_to(w_ref[k:k + 1, :], (SUBLANES, c))
    sub = lax.broadcasted_iota(jnp.int32, (SUBLANES, c), 0)
    bias = jnp.broadcast_to(b_ref[...], (SUBLANES, c))
    half = CONV_WIDTH // 2
    out_tiles = CONV_TB // SUBLANES
    win_tiles = (CONV_TB + 2 * CONV_PAD) // SUBLANES

    def block(i, carry):
        start = pl.multiple_of(i * CONV_TB, CONV_TB)
        tiles = [pad_ref[pl.ds(start + SUBLANES * m, SUBLANES), :] for m in range(win_tiles)]
        acc = [bias] * out_tiles
        for r in range(SUBLANES):
            sh = _shifted_tiles(tiles, r, win_tiles - 1, sub)
            for a in range(win_tiles - out_tiles):
                k = r + SUBLANES * a - (CONV_PAD - half)
                if 0 <= k < CONV_WIDTH:
                    w = w8_ref[k]
                    acc = [acc[j] + w * sh[j + a] for j in range(out_tiles)]
        conv_ref[pl.ds(start, CONV_TB), :] = jnp.concatenate(acc, axis=0)
        return carry

    lax.fori_loop(0, n // CONV_TB, block, 0)

    def norm_block(i, carry):
        start = pl.multiple_of(i * CONV_NORM_TB, CONV_NORM_TB)
        conv = conv_ref[pl.ds(start, CONV_NORM_TB), :]
        mu = jnp.mean(conv, axis=-1, keepdims=True)
        cen = conv - mu
        var = jnp.mean(cen * cen, axis=-1, keepdims=True)
        y = cen * lax.rsqrt(var + EPS) * lg_ref[...] + lb_ref[...]
        o_ref[0, pl.ds(start, CONV_NORM_TB), :] = (y * _sigmoid(y)).astype(o_ref.dtype)
        return carry

    lax.fori_loop(0, n // CONV_NORM_TB, norm_block, 0, unroll=4)


def _conformer(u, w, b, ln_g, ln_b):
    bsz, n, c = u.shape
    row = pl.BlockSpec((1, c), lambda i: (0, 0))
    return pl.pallas_call(
        functools.partial(_conformer_kernel, n=n),
        grid=(bsz,),
        in_specs=[pl.BlockSpec((1, n, c), lambda i: (i, 0, 0)),
                  pl.BlockSpec(w.shape, lambda i: (0, 0)), row, row, row],
        out_specs=pl.BlockSpec((1, n, c), lambda i: (i, 0, 0)),
        out_shape=jax.ShapeDtypeStruct((bsz, n, c), BF16),
        scratch_shapes=[pltpu.VMEM((n + 2 * CONV_PAD, c), F32), pltpu.VMEM((CONV_WIDTH, SUBLANES, c), F32),
                        pltpu.VMEM((n, c), F32)],
        compiler_params=_params("parallel"),
        name="conformer_conv",
    )(u, w, b.reshape(1, c), ln_g.reshape(1, c), ln_b.reshape(1, c))


def _head_of_lane():
    return lax.broadcasted_iota(jnp.int32, (1, HEAD_LANES), 1) // NA_HEAD_DIM


def _stack_heads(qr, head):
    zero = jnp.zeros_like(qr)
    return jnp.concatenate([jnp.where(head == h, qr, zero) for h in range(HEADS_PER_STEP)], axis=0)


def _unstack_heads(o, head, m):
    out = o[0:m]
    for h in range(1, HEADS_PER_STEP):
        out = jnp.where(head == h, o[h * m:(h + 1) * m], out)
    return out


def _build_na_bias(rpb_ref, bias_ref, group):
    qi = lax.broadcasted_iota(jnp.int32, (GRID_W, GRID_W), 0)
    ki = lax.broadcasted_iota(jnp.int32, (GRID_W, GRID_W), 1)
    rel = ki - qi
    col_start = jnp.clip(qi - NA_COLS // 2, 0, GRID_W - NA_COLS)
    valid = (ki >= col_start) & (ki < col_start + NA_COLS)
    n_r = 2 * NA_ROWS - 1
    n_c = 2 * NA_COLS - 1

    def per_head(h, carry):
        q0 = pl.multiple_of(h * GRID_W, GRID_W)
        for ri in range(n_r):
            base = ((group * HEADS_PER_STEP + h) * n_r + ri) * n_c
            t = jnp.full((GRID_W, GRID_W), NEG_INF, F32)
            for dc in range(n_c):
                t = jnp.where(rel == dc - (NA_COLS - 1), rpb_ref[base + dc], t)
            t = jnp.where(valid, t, NEG_INF)
            for d in range(NA_ROWS):
                j = ri + d - (NA_ROWS - 1)
                if 0 <= j < NA_ROWS:
                    bias_ref[d, pl.ds(q0, GRID_W), j * GRID_W:(j + 1) * GRID_W] = t
        return carry

    lax.fori_loop(0, HEADS_PER_STEP, per_head, 0)


def _lane_reduce(tiles, combine, reduce):
    by_width = {}
    for t in tiles:
        w = t.shape[-1]
        by_width[w] = t if w not in by_width else combine(by_width[w], t)
    parts = [reduce(t, axis=-1, keepdims=True) for t in by_width.values()]
    out = parts[0]
    for part in parts[1:]:
        out = combine(out, part)
    return out


def _na_latent_kernel(rpb_ref, q_ref, k_ref, v_ref, kc_ref, vc_ref, o_ref, bias_ref, *, rows, unroll):
    @pl.when(pl.program_id(1) == 0)
    def _():
        _build_na_bias(rpb_ref, bias_ref, pl.program_id(0))

    head = _head_of_lane()
    band = NA_ROWS * GRID_W
    nt = (((1,), (1,)), ((), ()))
    bounds = list(range(0, band, HEAD_LANES)) + [band]

    def row_step(r, carry):
        rs = jnp.clip(r - NA_ROWS // 2, 0, rows - NA_ROWS)
        q0 = pl.multiple_of(r * GRID_W, GRID_W)
        k0 = pl.multiple_of(rs * GRID_W, GRID_W)
        qs = _stack_heads(q_ref[0, pl.ds(q0, GRID_W), :], head)
        s_loc = lax.dot_general(qs, k_ref[0, pl.ds(k0, band), :], nt, preferred_element_type=F32) + bias_ref[r - rs]
        s_ctx = lax.dot_general(qs, kc_ref[0], nt, preferred_element_type=F32)
        tiles = [s_loc[:, lo:hi] for lo, hi in zip(bounds[:-1], bounds[1:])] + [s_ctx]
        m = _lane_reduce(tiles, jnp.maximum, jnp.max)
        probs = [jnp.exp(t - m) for t in tiles]
        inv = 1.0 / _lane_reduce(probs, jnp.add, jnp.sum)
        o = jnp.dot(jnp.concatenate(probs[:-1], axis=-1).astype(BF16), v_ref[0, pl.ds(k0, band), :],
                    preferred_element_type=F32)
        o = (o + jnp.dot(probs[-1].astype(BF16), vc_ref[0], preferred_element_type=F32)) * inv
        o_ref[0, pl.ds(q0, GRID_W), :] = _unstack_heads(o, head, GRID_W).astype(o_ref.dtype)
        return carry

    lax.fori_loop(0, rows, row_step, 0, unroll=unroll)


def _na_latent(q, k, v, kc, vc, rpb, *, unroll):
    bsz, n, na_dim = q.shape
    nc = kc.shape[1]
    groups = na_dim // HEAD_LANES
    rows = n // GRID_W
    assert rows >= NA_ROWS and rows % unroll == 0
    lat = pl.BlockSpec((1, n, HEAD_LANES), lambda g, i: (i, 0, g))
    ctx = pl.BlockSpec((1, nc, HEAD_LANES), lambda g, i: (i, 0, g))
    return pl.pallas_call(
        functools.partial(_na_latent_kernel, rows=rows, unroll=unroll),
        grid=(groups, bsz),
        in_specs=[pl.BlockSpec(memory_space=pltpu.SMEM), lat, lat, lat, ctx, ctx],
        out_specs=lat,
        out_shape=jax.ShapeDtypeStruct((bsz, n, na_dim), BF16),
        scratch_shapes=[pltpu.VMEM((NA_ROWS, HEADS_PER_STEP * GRID_W, NA_ROWS * GRID_W), F32)],
        compiler_params=_params("arbitrary", "arbitrary"),
        name=f"na_latent_unroll{unroll}",
    )(rpb.reshape(-1), q, k, v, kc, vc)


def _na_context_kernel(q_ref, k_ref, v_ref, o_ref):
    head = _head_of_lane()
    n = q_ref.shape[1]
    qs = _stack_heads(q_ref[0], head)
    s = lax.dot_general(qs, k_ref[0], (((1,), (1,)), ((), ())), preferred_element_type=F32)
    p = jnp.exp(s - jnp.max(s, axis=-1, keepdims=True))
    denom = jnp.sum(p, axis=-1, keepdims=True)
    o = jnp.dot(p.astype(BF16), v_ref[0], preferred_element_type=F32) * (1.0 / denom)
    o_ref[0] = _unstack_heads(o, head, n).astype(o_ref.dtype)


def _na_context(q, k, v):
    bsz, n, na_dim = q.shape
    blk = pl.BlockSpec((1, n, HEAD_LANES), lambda i, g: (i, 0, g))
    return pl.pallas_call(
        _na_context_kernel,
        grid=(bsz, na_dim // HEAD_LANES),
        in_specs=[blk, blk, blk],
        out_specs=blk,
        out_shape=jax.ShapeDtypeStruct((bsz, n, na_dim), BF16),
        compiler_params=_params("parallel", "parallel"),
        name="na_context",
    )(q, k, v)


LRU_TB = 256
LRU_PAD = 8


def _rglru_kernel(xl_ref, gl_ref, xc_ref, gc_ref, cw_ref, vec_ref, wx_ref, wa_ref, yl_ref, yc_ref,
                  padl_ref, padc_ref, a_ref, b_ref, h_ref, *, n_lat, n_ctx):
    c = xl_ref.shape[-1]
    zeros = jnp.zeros((LRU_PAD, c), F32)
    for pad_ref, src_ref, n in ((padc_ref, xc_ref, n_ctx), (padl_ref, xl_ref, n_lat)):
        pad_ref[0:LRU_PAD, :] = zeros
        pad_ref[n + LRU_PAD:n + 2 * LRU_PAD, :] = zeros
        pad_ref[LRU_PAD:n + LRU_PAD, :] = src_ref[0]
    sub = lax.broadcasted_iota(jnp.int32, (SUBLANES, c), 0)

    for direction in range(2):
        reverse = direction == 1
        conv_b = vec_ref[direction, 0:1, :]
        bx = vec_ref[direction, 1:2, :]
        ba = vec_ref[direction, 2:3, :]
        neg_lam = -vec_ref[direction, 3:4, :]
        softplus = jnp.maximum(neg_lam, 0.0) + jnp.log(1.0 + jnp.exp(-jnp.abs(neg_lam)))
        decay = -LRU_C * softplus

        conv_b8 = jnp.broadcast_to(conv_b, (SUBLANES, c))
        cw8 = [jnp.broadcast_to(cw_ref[direction, k:k + 1, :], (SUBLANES, c)) for k in range(LRU_CONV)]

        def coeffs(pad_ref, base, n):
            tb = min(n, LRU_TB)
            nt = tb // SUBLANES

            def block(i, carry):
                start = pl.multiple_of(i * tb, tb)
                tiles = [pad_ref[pl.ds(start + SUBLANES * m, SUBLANES), :]
                         for m in range(nt + 2 * LRU_PAD // SUBLANES)]
                u_tiles = [conv_b8] * nt
                for k in range(LRU_CONV):
                    off = LRU_PAD + k - (0 if reverse else LRU_CONV - 1)
                    sh = _shifted_tiles(tiles, off, nt, sub)
                    u_tiles = [u_tiles[j] + cw8[k] * sh[j] for j in range(nt)]
                u = jnp.concatenate(u_tiles, axis=0)
                ub = u.astype(BF16)
                gx = _sigmoid(jnp.dot(ub, wx_ref[direction], preferred_element_type=F32) + bx)
                ga = _sigmoid(jnp.dot(ub, wa_ref[direction], preferred_element_type=F32) + ba)
                a = jnp.exp(decay * ga)
                rem = 1.0 - a * a
                coef = jnp.where(rem > 0.0, rem * lax.rsqrt(rem), 0.0)
                a_ref[pl.ds(base + start, tb), :] = a
                b_ref[pl.ds(base + start, tb), :] = coef * (gx * u)
                return carry

            lax.fori_loop(0, n // tb, block, 0)

        coeffs(padc_ref, 0, n_ctx)
        coeffs(padl_ref, n_ctx, n_lat)

        def scan(base, n, hb):
            chunks = n // SUBLANES

            def chunk(j, hb):
                jj = chunks - 1 - j if reverse else j
                rows = pl.ds(pl.multiple_of(base + jj * SUBLANES, SUBLANES), SUBLANES)
                a = a_ref[rows, :]
                b = b_ref[rows, :]
                for d in (1, 2, 4):
                    shift = SUBLANES - d if reverse else d
                    keep = (sub < SUBLANES - d) if reverse else (sub >= d)
                    ra = pltpu.roll(a, shift, 0)
                    rb = pltpu.roll(b, shift, 0)
                    b = jnp.where(keep, a * rb + b, b)
                    a = jnp.where(keep, a * ra, a)
                h = a * hb + b
                if reverse:
                    h_ref[rows, :] = h_ref[rows, :] + h
                else:
                    h_ref[rows, :] = h
                last = 0 if reverse else SUBLANES - 1
                a_last = jnp.broadcast_to(a[last:last + 1, :], (SUBLANES, c))
                b_last = jnp.broadcast_to(b[last:last + 1, :], (SUBLANES, c))
                return a_last * hb + b_last

            return lax.fori_loop(0, chunks, chunk, hb, unroll=8)

        hb = scan(0, n_ctx, jnp.zeros((SUBLANES, c), F32))
        scan(n_ctx, n_lat, hb)

    def gate_out(g_ref, y_ref, base, n):
        tb = min(n, LRU_TB)

        def block(i, carry):
            start = pl.multiple_of(i * tb, tb)
            g = g_ref[0, pl.ds(start, tb), :]
            y_ref[0, pl.ds(start, tb), :] = (_gelu_tanh(g) * h_ref[pl.ds(base + start, tb), :]).astype(y_ref.dtype)
            return carry

        lax.fori_loop(0, n // tb, block, 0)

    gate_out(gc_ref, yc_ref, 0, n_ctx)
    gate_out(gl_ref, yl_ref, n_ctx, n_lat)


def _block_diag(w):
    two, nb, m, _ = w.shape
    eye = jnp.eye(nb, dtype=w.dtype)
    return (w[:, :, :, None, :] * eye[None, :, None, :, None]).reshape(two, nb * m, nb * m)


def _rglru(xl, gl, xc, gc, conv_w, conv_b, wx, bx, wa, ba, lam):
    bsz, n_lat, c = xl.shape
    n_ctx = xc.shape[1]
    vec = jnp.stack([conv_b, bx, ba, lam], axis=1)
    wxd = _block_diag(wx).astype(BF16)
    wad = _block_diag(wa).astype(BF16)
    lat = pl.BlockSpec((1, n_lat, c), lambda i: (i, 0, 0))
    ctx = pl.BlockSpec((1, n_ctx, c), lambda i: (i, 0, 0))
    full = lambda a: pl.BlockSpec(a.shape, lambda i: (0,) * a.ndim)
    return pl.pallas_call(
        functools.partial(_rglru_kernel, n_lat=n_lat, n_ctx=n_ctx),
        grid=(bsz,),
        in_specs=[lat, lat, ctx, ctx, full(conv_w), full(vec), full(wxd), full(wad)],
        out_specs=[lat, ctx],
        out_shape=[jax.ShapeDtypeStruct((bsz, n_lat, c), BF16), jax.ShapeDtypeStruct((bsz, n_ctx, c), BF16)],
        scratch_shapes=[pltpu.VMEM((n_lat + 2 * LRU_PAD, c), F32), pltpu.VMEM((n_ctx + 2 * LRU_PAD, c), F32),
                        pltpu.VMEM((n_ctx + n_lat, c), F32), pltpu.VMEM((n_ctx + n_lat, c), F32),
                        pltpu.VMEM((n_ctx + n_lat, c), F32)],
        compiler_params=_params("parallel"),
        name="rglru",
    )(xl, gl, xc, gc, conv_w, vec, wxd, wad)


FF_CHUNK = 1024


def _out_mlp_body(x, ya, yb, yc, mod_ref, g2_ref, gf_ref, wo_ref, w1_ref, w2_ref, final_norm):
    ca, cb, cc = ya.shape[-1], yb.shape[-1], yc.shape[-1]
    mix = (jnp.dot(ya, wo_ref[0:ca, :], preferred_element_type=F32)
           + jnp.dot(yb, wo_ref[ca:ca + cb, :], preferred_element_type=F32)
           + jnp.dot(yc, wo_ref[ca + cb:ca + cb + cc, :], preferred_element_type=F32))
    x1 = x + mod_ref[0, 2:3, :] * mix
    h2 = _rms(x1) * g2_ref[...]
    h2 = (h2 * (1.0 + mod_ref[0, 4:5, :]) + mod_ref[0, 3:4, :]).astype(BF16)
    acc = jnp.zeros(x1.shape, F32)
    for j in range(w1_ref.shape[1] // FF_CHUNK):
        cols = slice(j * FF_CHUNK, (j + 1) * FF_CHUNK)
        hid = jnp.maximum(jnp.dot(h2, w1_ref[:, cols], preferred_element_type=F32), 0.0)
        acc = acc + jnp.dot((hid * hid).astype(BF16), w2_ref[cols, :], preferred_element_type=F32)
    out = x1 + mod_ref[0, 5:6, :] * acc
    if final_norm:
        out = _rms(out) * gf_ref[...]
    return out


def _out_mlp_kernel(x_ref, ya_ref, yb_ref, yc_ref, mod_ref, g2_ref, gf_ref, wo_ref, w1_ref, w2_ref, o_ref, *,
                    final_norm):
    o_ref[0] = _out_mlp_body(x_ref[0], ya_ref[0], yb_ref[0], yc_ref[0], mod_ref, g2_ref, gf_ref,
                             wo_ref, w1_ref, w2_ref, final_norm)


def _conformer_tile(u_ref, w8_ref, bias, lg_ref, lb_ref, ya_ref, slot, tile, *, n, tm, sub):
    half = CONV_WIDTH // 2
    out_tiles = CONV_TB // SUBLANES
    win_tiles = (CONV_TB + 2 * CONV_PAD) // SUBLANES
    pad_tiles = CONV_PAD // SUBLANES
    blocks = tm // CONV_TB
    first = tile == 0
    last = tile == n // tm - 1
    zero = jnp.zeros((SUBLANES, u_ref.shape[-1]), F32)
    done = []
    for blk in range(blocks):
        base = tile * tm + (CONV_TB * blk - CONV_PAD)
        tiles = []
        for m in range(win_tiles):
            start = base + SUBLANES * m
            if blk == 0 and m < pad_tiles:
                t = u_ref[0, pl.ds(pl.multiple_of(jnp.maximum(start, 0), SUBLANES), SUBLANES), :]
                t = jnp.where(first, zero, t)
            elif blk == blocks - 1 and m >= win_tiles - pad_tiles:
                t = u_ref[0, pl.ds(pl.multiple_of(jnp.minimum(start, n - SUBLANES), SUBLANES), SUBLANES), :]
                t = jnp.where(last, zero, t)
            else:
                t = u_ref[0, pl.ds(pl.multiple_of(start, SUBLANES), SUBLANES), :]
            tiles.append(t)
        acc = [bias] * out_tiles
        for r in range(SUBLANES):
            sh = _shifted_tiles(tiles, r, win_tiles - 1, sub)
            for a in range(win_tiles - out_tiles):
                k = r + SUBLANES * a - (CONV_PAD - half)
                if 0 <= k < CONV_WIDTH:
                    w = w8_ref[k]
                    acc = [acc[j] + w * sh[j + a] for j in range(out_tiles)]
        done.extend(acc)
        if len(done) == CONV_NORM_TB // SUBLANES:
            conv = jnp.concatenate(done, axis=0)
            done = []
            mu = jnp.mean(conv, axis=-1, keepdims=True)
            cen = conv - mu
            var = jnp.mean(cen * cen, axis=-1, keepdims=True)
            y = cen * lax.rsqrt(var + EPS) * lg_ref[...] + lb_ref[...]
            row0 = CONV_TB * (blk + 1) - CONV_NORM_TB
            ya_ref[slot, row0:row0 + CONV_NORM_TB, :] = (y * _sigmoid(y)).astype(ya_ref.dtype)


def _conv_out_mlp_kernel(x_ref, u_ref, yb_ref, yc_ref, cw_ref, cb_ref, lg_ref, lb_ref, mod_ref, g2_ref, gf_ref,
                         wo_ref, w1_ref, w2_ref, o_ref, ya_ref, w8_ref, *, final_norm, n, tm):
    c = u_ref.shape[-1]
    tiles_per_seq = n // tm
    step = pl.program_id(0) * tiles_per_seq + pl.program_id(1)
    cur = step % 2
    for k in range(CONV_WIDTH):
        w8_ref[k] = jnp.broadcast_to(cw_ref[k:k + 1, :], (SUBLANES, c))
    sub = lax.broadcasted_iota(jnp.int32, (SUBLANES, c), 0)
    bias = jnp.broadcast_to(cb_ref[...], (SUBLANES, c))
    conv = functools.partial(_conformer_tile, u_ref, w8_ref, bias, lg_ref, lb_ref, ya_ref, n=n, tm=tm, sub=sub)

    @pl.when(step == 0)
    def _():
        conv(0, 0)

    o_ref[0] = _out_mlp_body(x_ref[0], ya_ref[cur], yb_ref[0], yc_ref[0], mod_ref, g2_ref, gf_ref,
                             wo_ref, w1_ref, w2_ref, final_norm)
    conv(1 - cur, (pl.program_id(1) + 1) % tiles_per_seq)


def _out_mlp(x, ya, yb, yc, mod, g2, gf, wo, w1, w2, *, final_norm):
    b, n, d = x.shape
    tm = min(n, 512)
    per_batch = mod.shape[0] != 1
    tok = lambda a: pl.BlockSpec((1, tm, a.shape[-1]), lambda i, j: (i, j, 0))
    resident = lambda a: pl.BlockSpec(a.shape, lambda i, j: (0, 0), pipeline_mode=pl.Buffered(1))
    return pl.pallas_call(
        functools.partial(_out_mlp_kernel, final_norm=final_norm),
        grid=(b, n // tm),
        in_specs=[tok(x), tok(ya), tok(yb), tok(yc),
                  pl.BlockSpec((1, 6, d), (lambda i, j: (i, 0, 0)) if per_batch else (lambda i, j: (0, 0, 0))),
                  pl.BlockSpec((1, d), lambda i, j: (0, 0)), pl.BlockSpec((1, d), lambda i, j: (0, 0)),
                  resident(wo), resident(w1), resident(w2)],
        out_specs=tok(x),
        out_shape=jax.ShapeDtypeStruct((b, n, d), F32),
        compiler_params=_params("parallel", "parallel"),
        name="out_proj_mlp",
    )(x, ya, yb, yc, mod, g2, gf, wo, w1, w2)


def _conv_out_mlp(x, u, yb, yc, conv_w, conv_b, ln_g, ln_b, mod, g2, gf, wo, w1, w2, *, final_norm):
    b, n, d = x.shape
    c = u.shape[-1]
    tm = min(n, 512)
    nt = n // tm
    assert tm % CONV_NORM_TB == 0 and CONV_NORM_TB % CONV_TB == 0
    tok = lambda a: pl.BlockSpec((1, tm, a.shape[-1]), lambda i, j: (i, j, 0))
    row = lambda width: pl.BlockSpec((1, width), lambda i, j: (0, 0))
    resident = lambda a: pl.BlockSpec(a.shape, lambda i, j: (0, 0), pipeline_mode=pl.Buffered(1))
    next_batch = lambda i, j: (jnp.minimum(i + (j + 1) // nt, b - 1), 0, 0)
    return pl.pallas_call(
        functools.partial(_conv_out_mlp_kernel, final_norm=final_norm, n=n, tm=tm),
        grid=(b, nt),
        in_specs=[tok(x), pl.BlockSpec((1, n, c), next_batch), tok(yb), tok(yc),
                  pl.BlockSpec(conv_w.shape, lambda i, j: (0, 0)), row(c), row(c), row(c),
                  pl.BlockSpec((1, 6, d), lambda i, j: (i, 0, 0)), row(d), row(d),
                  resident(wo), resident(w1), resident(w2)],
        out_specs=tok(x),
        out_shape=jax.ShapeDtypeStruct((b, n, d), F32),
        scratch_shapes=[pltpu.VMEM((2, tm, c), BF16), pltpu.VMEM((CONV_WIDTH, SUBLANES, c), F32)],
        compiler_params=_params("arbitrary", "arbitrary"),
        name="conv_out_proj_mlp",
    )(x, u, yb, yc, conv_w, conv_b.reshape(1, c), ln_g.reshape(1, c), ln_b.reshape(1, c), mod, g2, gf, wo, w1, w2)


def kernel(x, c, ctx, c_ctx, norm1_g, norm2_g, ada_w, ada_b, w_in, w_out, conv_w, conv_b, conv_ln_g, conv_ln_b,
           na_rpb, lru_conv_w, lru_conv_b, lru_wx, lru_bx, lru_wa, lru_ba, lru_lambda, mlp_w1, mlp_w2, final_g):
    depth = w_in.shape[0]
    bsz, _, d = x.shape
    conv_dim = conv_w.shape[-1]
    lru_dim = lru_conv_w.shape[-1]
    na_dim = NA_HEADS * NA_HEAD_DIM
    dims = dict(conv_dim=conv_dim, na_dim=na_dim, lru_dim=lru_dim)

    cond = jnp.concatenate([c, c_ctx[None, :]], axis=0)
    cond = jnp.pad(cond, ((0, -(bsz + 1) % SUBLANES), (0, 0)))
    mod_all = _modulation(cond, ada_w, ada_b)
    cx = ctx
    for l in range(depth):
        last = l == depth - 1
        mod = mod_all[l, :bsz].reshape(bsz, 6, d)
        mod_c = mod_all[l, bsz:bsz + 1].reshape(1, 6, d)
        w_in_l = w_in[l].astype(BF16)
        g1 = norm1_g[l].reshape(1, d)

        u, q, k, v, rx, rg = _in_proj(x, mod, g1, w_in_l, **dims)
        cu, cq, ck, cv, crx, crg = _in_proj(cx, mod_c, g1, w_in_l, **dims)

        y_b = _na_latent(q, k, v, ck, cv, na_rpb[l], unroll=16 if l % 2 == 0 else 8)
        y_c, yc_c = _rglru(rx, rg, crx, crg, lru_conv_w[l], lru_conv_b[l], lru_wx[l], lru_bx[l],
                           lru_wa[l], lru_ba[l], lru_lambda[l])

        w_out_l = w_out[l].astype(BF16)
        w1_l = mlp_w1[l].astype(BF16)
        w2_l = mlp_w2[l].astype(BF16)
        g2 = norm2_g[l].reshape(1, d)
        gf = final_g.reshape(1, d)
        x = _conv_out_mlp(x, u, y_b, y_c, conv_w[l], conv_b[l], conv_ln_g[l], conv_ln_b[l], mod, g2, gf,
                          w_out_l, w1_l, w2_l, final_norm=last)
        if not last:
            yc_a = _conformer(cu, conv_w[l], conv_b[l], conv_ln_g[l], conv_ln_b[l])
            yc_b = _na_context(cq, ck, cv)
            cx = _out_mlp(cx, yc_a, yc_b, yc_c, mod_c, g2, gf, w_out_l, w1_l, w2_l, final_norm=False)
    return x
```

```python
import functools

import jax
import jax.numpy as jnp
from jax import lax
from jax.experimental import pallas as pl
from jax.experimental.pallas import tpu as pltpu

GRID_W = 64
CONV_WIDTH = 31
NA_HEADS = 8
NA_HEAD_DIM = 64
NA_ROWS = 8
NA_COLS = 16
LRU_BLOCKS = 4
LRU_CONV = 4
LRU_C = 8.0
EPS = 1e-6
NEG_INF = -1e30

SUBLANES = 8
LANES = 128
HEADS_PER_STEP = 4
HEAD_LANES = HEADS_PER_STEP * NA_HEAD_DIM
VMEM_LIMIT_BYTES = 56 * 1024 * 1024

F32 = jnp.float32
BF16 = jnp.bfloat16


def _sigmoid(x):
    return 1.0 / (1.0 + jnp.exp(-x))


def _gelu_tanh(x):
    return 0.5 * x * (1.0 + jnp.tanh(0.7978845608028654 * (x + 0.044715 * (x * x * x))))


def _rms(x):
    return x * lax.rsqrt(jnp.mean(x * x, axis=-1, keepdims=True) + EPS)


def _params(*sem):
    return pltpu.CompilerParams(dimension_semantics=sem, vmem_limit_bytes=VMEM_LIMIT_BYTES)


def _mod_kernel(c_ref, w_ref, b_ref, o_ref):
    c = c_ref[...]
    s = c * _sigmoid(c)
    o_ref[0] = jnp.dot(s, w_ref[0], preferred_element_type=F32,
                       precision=lax.Precision.HIGHEST) + b_ref[0]


def _modulation(cc, ada_w, ada_b):
    depth, d, n = ada_w.shape
    r = cc.shape[0]
    tn = n // 4
    return pl.pallas_call(
        _mod_kernel,
        grid=(depth, n // tn),
        in_specs=[pl.BlockSpec((r, d), lambda l, j: (0, 0)),
                  pl.BlockSpec((1, d, tn), lambda l, j: (l, 0, j)),
                  pl.BlockSpec((1, 1, tn), lambda l, j: (l, 0, j))],
        out_specs=pl.BlockSpec((1, r, tn), lambda l, j: (l, 0, j)),
        out_shape=jax.ShapeDtypeStruct((depth, r, n), F32),
        compiler_params=_params("parallel", "parallel"),
        name="adaln_modulation",
    )(cc, ada_w, ada_b.reshape(depth, 1, n))


IN_PROJ_OUTPUTS = ("u", "q", "k", "v", "rx", "rg")


def _in_proj_kernel(x_ref, mod_ref, g_ref, w_ref, *out_refs, conv_dim, na_dim, lru_dim, want):
    out = dict(zip(want, out_refs))
    x = x_ref[0]
    h = _rms(x) * g_ref[...]
    h = h * (1.0 + mod_ref[0, 1:2, :]) + mod_ref[0, 0:1, :]
    hb = h.astype(BF16)

    def proj(off, width):
        return jnp.dot(hb, w_ref[:, off:off + width], preferred_element_type=F32)

    q_off = 2 * conv_dim
    if "u" in out:
        out["u"][0] = proj(0, conv_dim) * _sigmoid(proj(conv_dim, conv_dim))
    if "q" in out:
        out["q"][0] = (proj(q_off, na_dim) * (NA_HEAD_DIM ** -0.5)).astype(BF16)
    if "k" in out:
        out["k"][0] = proj(q_off + na_dim, na_dim).astype(BF16)
    if "v" in out:
        out["v"][0] = proj(q_off + 2 * na_dim, na_dim).astype(BF16)
    if "rx" in out:
        out["rx"][0] = proj(q_off + 3 * na_dim, lru_dim)
    if "rg" in out:
        out["rg"][0] = proj(q_off + 3 * na_dim + lru_dim, lru_dim)


IN_PROJ_TM = 1024


def _in_proj(x, mod, g, w_bf16, *, conv_dim, na_dim, lru_dim, want=IN_PROJ_OUTPUTS):
    b, n, d = x.shape
    tm = min(n, IN_PROJ_TM)
    per_batch = mod.shape[0] != 1
    kinds = dict(u=(conv_dim, F32), q=(na_dim, BF16), k=(na_dim, BF16), v=(na_dim, BF16),
                 rx=(lru_dim, F32), rg=(lru_dim, F32))
    outs = pl.pallas_call(
        functools.partial(_in_proj_kernel, conv_dim=conv_dim, na_dim=na_dim, lru_dim=lru_dim, want=want),
        grid=(b, n // tm),
        in_specs=[pl.BlockSpec((1, tm, d), lambda i, j: (i, j, 0)),
                  pl.BlockSpec((1, 6, d), (lambda i, j: (i, 0, 0)) if per_batch else (lambda i, j: (0, 0, 0))),
                  pl.BlockSpec((1, d), lambda i, j: (0, 0)),
                  pl.BlockSpec(w_bf16.shape, lambda i, j: (0, 0))],
        out_specs=[pl.BlockSpec((1, tm, kinds[name][0]), lambda i, j: (i, j, 0)) for name in want],
        out_shape=[jax.ShapeDtypeStruct((b, n, kinds[name][0]), kinds[name][1]) for name in want],
        compiler_params=_params("parallel", "parallel"),
        name="in_proj",
    )(x, mod, g, w_bf16)
    return dict(zip(want, outs))


CONV_TB = 32
CONV_PAD = 16
CONV_NORM_TB = 128


def _shifted_tiles(tiles, off, count, sub):
    a, r = divmod(off, SUBLANES)
    if r == 0:
        return tiles[a:a + count]
    rolled = [pltpu.roll(t, SUBLANES - r, 0) for t in tiles[a:a + count + 1]]
    keep = sub < SUBLANES - r
    return [jnp.where(keep, rolled[i], rolled[i + 1]) for i in range(count)]


def _conformer_kernel(u_ref, w_ref, b_ref, lg_ref, lb_ref, o_ref, pad_ref, w8_ref, conv_ref, *, n):
    c = u_ref.shape[-1]
    zeros = jnp.zeros((CONV_PAD, c), F32)
    pad_ref[0:CONV_PAD, :] = zeros
    pad_ref[n + CONV_PAD:n + 2 * CONV_PAD, :] = zeros
    pad_ref[CONV_PAD:n + CONV_PAD, :] = u_ref[0]
    for k in range(CONV_WIDTH):
        w8_ref[k] = jnp.broadcast_to(w_ref[k:k + 1, :], (SUBLANES, c))
    sub = lax.broadcasted_iota(jnp.int32, (SUBLANES, c), 0)
    bias = jnp.broadcast_to(b_ref[...], (SUBLANES, c))
    half = CONV_WIDTH // 2
    out_tiles = CONV_TB // SUBLANES
    win_tiles = (CONV_TB + 2 * CONV_PAD) // SUBLANES

    def block(i, carry):
        start = pl.multiple_of(i * CONV_TB, CONV_TB)
        tiles = [pad_ref[pl.ds(start + SUBLANES * m, SUBLANES), :] for m in range(win_tiles)]
        acc = [bias] * out_tiles
        for r in range(SUBLANES):
            sh = _shifted_tiles(tiles, r, win_tiles - 1, sub)
            for a in range(win_tiles - out_tiles):
                k = r + SUBLANES * a - (CONV_PAD - half)
                if 0 <= k < CONV_WIDTH:
                    w = w8_ref[k]
                    acc = [acc[j] + w * sh[j + a] for j in range(out_tiles)]
        conv_ref[pl.ds(start, CONV_TB), :] = jnp.concatenate(acc, axis=0)
        return carry

    lax.fori_loop(0, n // CONV_TB, block, 0)

    def norm_block(i, carry):
        start = pl.multiple_of(i * CONV_NORM_TB, CONV_NORM_TB)
        conv = conv_ref[pl.ds(start, CONV_NORM_TB), :]
        mu = jnp.mean(conv, axis=-1, keepdims=True)
        cen = conv - mu
        var = jnp.mean(cen * cen, axis=-1, keepdims=True)
        y = cen * lax.rsqrt(var + EPS) * lg_ref[...] + lb_ref[...]
        o_ref[0, pl.ds(start, CONV_NORM_TB), :] = (y * _sigmoid(y)).astype(o_ref.dtype)
        return carry

    lax.fori_loop(0, n // CONV_NORM_TB, norm_block, 0, unroll=4)


def _conformer(u, w, b, ln_g, ln_b):
    bsz, n, c = u.shape
    row = pl.BlockSpec((1, c), lambda i: (0, 0))
    return pl.pallas_call(
        functools.partial(_conformer_kernel, n=n),
        grid=(bsz,),
        in_specs=[pl.BlockSpec((1, n, c), lambda i: (i, 0, 0)),
                  pl.BlockSpec(w.shape, lambda i: (0, 0)), row, row, row],
        out_specs=pl.BlockSpec((1, n, c), lambda i: (i, 0, 0)),
        out_shape=jax.ShapeDtypeStruct((bsz, n, c), BF16),
        scratch_shapes=[pltpu.VMEM((n + 2 * CONV_PAD, c), F32), pltpu.VMEM((CONV_WIDTH, SUBLANES, c), F32),
                        pltpu.VMEM((n, c), F32)],
        compiler_params=_params("parallel"),
        name="conformer_conv",
    )(u, w, b.reshape(1, c), ln_g.reshape(1, c), ln_b.reshape(1, c))


def _head_of_lane():
    return lax.broadcasted_iota(jnp.int32, (1, HEAD_LANES), 1) // NA_HEAD_DIM


def _stack_heads(qr, head):
    zero = jnp.zeros_like(qr)
    return jnp.concatenate([jnp.where(head == h, qr, zero) for h in range(HEADS_PER_STEP)], axis=0)


def _unstack_heads(o, head, m):
    out = o[0:m]
    for h in range(1, HEADS_PER_STEP):
        out = jnp.where(head == h, o[h * m:(h + 1) * m], out)
    return out


def _build_na_bias(rpb_ref, bias_ref, group):
    qi = lax.broadcasted_iota(jnp.int32, (GRID_W, GRID_W), 0)
    ki = lax.broadcasted_iota(jnp.int32, (GRID_W, GRID_W), 1)
    rel = ki - qi
    col_start = jnp.clip(qi - NA_COLS // 2, 0, GRID_W - NA_COLS)
    valid = (ki >= col_start) & (ki < col_start + NA_COLS)
    n_r = 2 * NA_ROWS - 1
    n_c = 2 * NA_COLS - 1

    def per_head(h, carry):
        q0 = pl.multiple_of(h * GRID_W, GRID_W)
        for ri in range(n_r):
            base = ((group * HEADS_PER_STEP + h) * n_r + ri) * n_c
            t = jnp.full((GRID_W, GRID_W), NEG_INF, F32)
            for dc in range(n_c):
                t = jnp.where(rel == dc - (NA_COLS - 1), rpb_ref[base + dc], t)
            t = jnp.where(valid, t, NEG_INF)
            for d in range(NA_ROWS):
                j = ri + d - (NA_ROWS - 1)
                if 0 <= j < NA_ROWS:
                    bias_ref[d, pl.ds(q0, GRID_W), j * GRID_W:(j + 1) * GRID_W] = t
        return carry

    lax.fori_loop(0, HEADS_PER_STEP, per_head, 0)


def _lane_reduce(tiles, combine, reduce):
    by_width = {}
    for t in tiles:
        w = t.shape[-1]
        by_width[w] = t if w not in by_width else combine(by_width[w], t)
    parts = [reduce(t, axis=-1, keepdims=True) for t in by_width.values()]
    out = parts[0]
    for part in parts[1:]:
        out = combine(out, part)
    return out


def _na_latent_kernel(rpb_ref, q_ref, k_ref, v_ref, kc_ref, vc_ref, o_ref, bias_ref, *, rows, unroll):
    @pl.when(pl.program_id(1) == 0)
    def _():
        _build_na_bias(rpb_ref, bias_ref, pl.program_id(0))

    head = _head_of_lane()
    band = NA_ROWS * GRID_W
    nt = (((1,), (1,)), ((), ()))
    bounds = list(range(0, band, HEAD_LANES)) + [band]

    def row_step(r, carry):
        rs = jnp.clip(r - NA_ROWS // 2, 0, rows - NA_ROWS)
        q0 = pl.multiple_of(r * GRID_W, GRID_W)
        k0 = pl.multiple_of(rs * GRID_W, GRID_W)
        qs = _stack_heads(q_ref[0, pl.ds(q0, GRID_W), :], head)
        s_loc = lax.dot_general(qs, k_ref[0, pl.ds(k0, band), :], nt, preferred_element_type=F32) + bias_ref[r - rs]
        s_ctx = lax.dot_general(qs, kc_ref[0], nt, preferred_element_type=F32)
        tiles = [s_loc[:, lo:hi] for lo, hi in zip(bounds[:-1], bounds[1:])] + [s_ctx]
        m = _lane_reduce(tiles, jnp.maximum, jnp.max)
        probs = [jnp.exp(t - m) for t in tiles]
        inv = 1.0 / _lane_reduce(probs, jnp.add, jnp.sum)
        o = jnp.dot(jnp.concatenate(probs[:-1], axis=-1).astype(BF16), v_ref[0, pl.ds(k0, band), :],
                    preferred_element_type=F32)
        o = (o + jnp.dot(probs[-1].astype(BF16), vc_ref[0], preferred_element_type=F32)) * inv
        o_ref[0, pl.ds(q0, GRID_W), :] = _unstack_heads(o, head, GRID_W).astype(o_ref.dtype)
        return carry

    lax.fori_loop(0, rows, row_step, 0, unroll=unroll)


def _na_latent(q, k, v, kc, vc, rpb, *, unroll):
    bsz, n, na_dim = q.shape
    nc = kc.shape[1]
    groups = na_dim // HEAD_LANES
    rows = n // GRID_W
    unroll = min(unroll, rows)
    assert rows >= NA_ROWS and rows % unroll == 0
    lat = pl.BlockSpec((1, n, HEAD_LANES), lambda g, i: (i, 0, g))
    ctx = pl.BlockSpec((1, nc, HEAD_LANES), lambda g, i: (i, 0, g))
    return pl.pallas_call(
        functools.partial(_na_latent_kernel, rows=rows, unroll=unroll),
        grid=(groups, bsz),
        in_specs=[pl.BlockSpec(memory_space=pltpu.SMEM), lat, lat, lat, ctx, ctx],
        out_specs=lat,
        out_shape=jax.ShapeDtypeStruct((bsz, n, na_dim), BF16),
        scratch_shapes=[pltpu.VMEM((NA_ROWS, HEADS_PER_STEP * GRID_W, NA_ROWS * GRID_W), F32)],
        compiler_params=_params("arbitrary", "arbitrary"),
        name=f"na_latent_unroll{unroll}",
    )(rpb.reshape(-1), q, k, v, kc, vc)


def _na_context_kernel(q_ref, k_ref, v_ref, o_ref):
    head = _head_of_lane()
    n = q_ref.shape[1]
    qs = _stack_heads(q_ref[0], head)
    s = lax.dot_general(qs, k_ref[0], (((1,), (1,)), ((), ())), preferred_element_type=F32)
    p = jnp.exp(s - jnp.max(s, axis=-1, keepdims=True))
    denom = jnp.sum(p, axis=-1, keepdims=True)
    o = jnp.dot(p.astype(BF16), v_ref[0], preferred_element_type=F32) * (1.0 / denom)
    o_ref[0] = _unstack_heads(o, head, n).astype(o_ref.dtype)


def _na_context(q, k, v):
    bsz, n, na_dim = q.shape
    blk = pl.BlockSpec((1, n, HEAD_LANES), lambda i, g: (i, 0, g))
    return pl.pallas_call(
        _na_context_kernel,
        grid=(bsz, na_dim // HEAD_LANES),
        in_specs=[blk, blk, blk],
        out_specs=blk,
        out_shape=jax.ShapeDtypeStruct((bsz, n, na_dim), BF16),
        compiler_params=_params("parallel", "parallel"),
        name="na_context",
    )(q, k, v)


LRU_TB = 256
LRU_PAD = 8


def _rglru_kernel(xl_ref, gl_ref, xc_ref, gc_ref, cw_ref, vec_ref, wx_ref, wa_ref, yl_ref, yc_ref,
                  padl_ref, padc_ref, a_ref, b_ref, h_ref, *, n_lat, n_ctx):
    c = xl_ref.shape[-1]
    zeros = jnp.zeros((LRU_PAD, c), F32)
    for pad_ref, src_ref, n in ((padc_ref, xc_ref, n_ctx), (padl_ref, xl_ref, n_lat)):
        pad_ref[0:LRU_PAD, :] = zeros
        pad_ref[n + LRU_PAD:n + 2 * LRU_PAD, :] = zeros
        pad_ref[LRU_PAD:n + LRU_PAD, :] = src_ref[0]
    sub = lax.broadcasted_iota(jnp.int32, (SUBLANES, c), 0)

    for direction in range(2):
        reverse = direction == 1
        conv_b = vec_ref[direction, 0:1, :]
        bx = vec_ref[direction, 1:2, :]
        ba = vec_ref[direction, 2:3, :]
        neg_lam = -vec_ref[direction, 3:4, :]
        softplus = jnp.maximum(neg_lam, 0.0) + jnp.log(1.0 + jnp.exp(-jnp.abs(neg_lam)))
        decay = -LRU_C * softplus

        conv_b8 = jnp.broadcast_to(conv_b, (SUBLANES, c))
        cw8 = [jnp.broadcast_to(cw_ref[direction, k:k + 1, :], (SUBLANES, c)) for k in range(LRU_CONV)]

        def coeffs(pad_ref, base, n):
            tb = min(n, LRU_TB)
            nt = tb // SUBLANES

            def block(i, carry):
                start = pl.multiple_of(i * tb, tb)
                tiles = [pad_ref[pl.ds(start + SUBLANES * m, SUBLANES), :]
                         for m in range(nt + 2 * LRU_PAD // SUBLANES)]
                u_tiles = [conv_b8] * nt
                for k in range(LRU_CONV):
                    off = LRU_PAD + k - (0 if reverse else LRU_CONV - 1)
                    sh = _shifted_tiles(tiles, off, nt, sub)
                    u_tiles = [u_tiles[j] + cw8[k] * sh[j] for j in range(nt)]
                u = jnp.concatenate(u_tiles, axis=0)
                ub = u.astype(BF16)
                gx = 1.0 / (1.0 + jnp.exp(jnp.dot(ub, wx_ref[direction], preferred_element_type=F32) - bx))
                ga = 1.0 / (1.0 + jnp.exp(jnp.dot(ub, wa_ref[direction], preferred_element_type=F32) - ba))
                a = jnp.exp(decay * ga)
                rem = 1.0 - a * a
                coef = jnp.where(rem > 0.0, rem * lax.rsqrt(rem), 0.0)
                a_ref[pl.ds(base + start, tb), :] = a
                b_ref[pl.ds(base + start, tb), :] = coef * (gx * u)
                return carry

            lax.fori_loop(0, n // tb, block, 0)

        coeffs(padc_ref, 0, n_ctx)
        coeffs(padl_ref, n_ctx, n_lat)

        def scan(base, n, hb):
            chunks = n // SUBLANES

            def chunk(j, hb):
                jj = chunks - 1 - j if reverse else j
                rows = pl.ds(pl.multiple_of(base + jj * SUBLANES, SUBLANES), SUBLANES)
                a = a_ref[rows, :]
                b = b_ref[rows, :]
                for d in (1, 2, 4):
                    shift = SUBLANES - d if reverse else d
                    keep = (sub < SUBLANES - d) if reverse else (sub >= d)
                    ra = pltpu.roll(a, shift, 0)
                    rb = pltpu.roll(b, shift, 0)
                    b = jnp.where(keep, a * rb + b, b)
                    a = jnp.where(keep, a * ra, a)
                h = a * hb + b
                if reverse:
                    h_ref[rows, :] = h_ref[rows, :] + h
                else:
                    h_ref[rows, :] = h
                last = 0 if reverse else SUBLANES - 1
                a_last = jnp.broadcast_to(a[last:last + 1, :], (SUBLANES, c))
                b_last = jnp.broadcast_to(b[last:last + 1, :], (SUBLANES, c))
                return a_last * hb + b_last

            return lax.fori_loop(0, chunks, chunk, hb, unroll=8)

        hb = scan(0, n_ctx, jnp.zeros((SUBLANES, c), F32))
        scan(n_ctx, n_lat, hb)

    def gate_out(g_ref, y_ref, base, n):
        tb = min(n, LRU_TB)

        def block(i, carry):
            start = pl.multiple_of(i * tb, tb)
            g = g_ref[0, pl.ds(start, tb), :]
            y_ref[0, pl.ds(start, tb), :] = (_gelu_tanh(g) * h_ref[pl.ds(base + start, tb), :]).astype(y_ref.dtype)
            return carry

        lax.fori_loop(0, n // tb, block, 0)

    gate_out(gc_ref, yc_ref, 0, n_ctx)
    gate_out(gl_ref, yl_ref, n_ctx, n_lat)


def _block_diag(w):
    two, nb, m, _ = w.shape
    eye = jnp.eye(nb, dtype=w.dtype)
    return (w[:, :, :, None, :] * eye[None, :, None, :, None]).reshape(two, nb * m, nb * m)


def _rglru(xl, gl, xc, gc, conv_w, conv_b, wx, bx, wa, ba, lam):
    bsz, n_lat, c = xl.shape
    n_ctx = xc.shape[1]
    vec = jnp.stack([conv_b, bx, ba, lam], axis=1)
    wxd = _block_diag(-wx).astype(BF16)
    wad = _block_diag(-wa).astype(BF16)
    lat = pl.BlockSpec((1, n_lat, c), lambda i: (i, 0, 0))
    ctx = pl.BlockSpec((1, n_ctx, c), lambda i: (i, 0, 0))
    full = lambda a: pl.BlockSpec(a.shape, lambda i: (0,) * a.ndim)
    return pl.pallas_call(
        functools.partial(_rglru_kernel, n_lat=n_lat, n_ctx=n_ctx),
        grid=(bsz,),
        in_specs=[lat, lat, ctx, ctx, full(conv_w), full(vec), full(wxd), full(wad)],
        out_specs=[lat, ctx],
        out_shape=[jax.ShapeDtypeStruct((bsz, n_lat, c), BF16), jax.ShapeDtypeStruct((bsz, n_ctx, c), BF16)],
        scratch_shapes=[pltpu.VMEM((n_lat + 2 * LRU_PAD, c), F32), pltpu.VMEM((n_ctx + 2 * LRU_PAD, c), F32),
                        pltpu.VMEM((n_ctx + n_lat, c), F32), pltpu.VMEM((n_ctx + n_lat, c), F32),
                        pltpu.VMEM((n_ctx + n_lat, c), F32)],
        compiler_params=_params("parallel"),
        name="rglru",
    )(xl, gl, xc, gc, conv_w, vec, wxd, wad)


FF_CHUNK = 1024


def _out_mlp_body(x, ya, yb, yc, mod_ref, g2_ref, gf_ref, wo_ref, w1_ref, w2_ref, final_norm):
    ca, cb, cc = ya.shape[-1], yb.shape[-1], yc.shape[-1]
    mix = (jnp.dot(ya, wo_ref[0:ca, :], preferred_element_type=F32)
           + jnp.dot(yb, wo_ref[ca:ca + cb, :], preferred_element_type=F32)
           + jnp.dot(yc, wo_ref[ca + cb:ca + cb + cc, :], preferred_element_type=F32))
    x1 = x + mod_ref[0, 2:3, :] * mix
    h2 = _rms(x1) * g2_ref[...]
    h2 = (h2 * (1.0 + mod_ref[0, 4:5, :]) + mod_ref[0, 3:4, :]).astype(BF16)
    acc = jnp.zeros(x1.shape, F32)
    for j in range(w1_ref.shape[1] // FF_CHUNK):
        cols = slice(j * FF_CHUNK, (j + 1) * FF_CHUNK)
        hid = jnp.maximum(jnp.dot(h2, w1_ref[:, cols], preferred_element_type=F32), 0.0)
        acc = acc + jnp.dot((hid * hid).astype(BF16), w2_ref[cols, :], preferred_element_type=F32)
    out = x1 + mod_ref[0, 5:6, :] * acc
    if final_norm:
        out = _rms(out) * gf_ref[...]
    return out


def _out_mlp_kernel(x_ref, ya_ref, yb_ref, yc_ref, mod_ref, g2_ref, gf_ref, wo_ref, w1_ref, w2_ref, o_ref, *,
                    final_norm):
    o_ref[0] = _out_mlp_body(x_ref[0], ya_ref[0], yb_ref[0], yc_ref[0], mod_ref, g2_ref, gf_ref,
                             wo_ref, w1_ref, w2_ref, final_norm)


def _conformer_tile(u_ref, w8_ref, bias, lg_ref, lb_ref, ya_ref, slot, tile, *, n, tm, sub):
    half = CONV_WIDTH // 2
    out_tiles = CONV_TB // SUBLANES
    win_tiles = (CONV_TB + 2 * CONV_PAD) // SUBLANES
    pad_tiles = CONV_PAD // SUBLANES
    blocks = tm // CONV_TB
    first = tile == 0
    last = tile == n // tm - 1
    zero = jnp.zeros((SUBLANES, u_ref.shape[-1]), F32)
    done = []
    for blk in range(blocks):
        base = tile * tm + (CONV_TB * blk - CONV_PAD)
        tiles = []
        for m in range(win_tiles):
            start = base + SUBLANES * m
            if blk == 0 and m < pad_tiles:
                t = u_ref[0, pl.ds(pl.multiple_of(jnp.maximum(start, 0), SUBLANES), SUBLANES), :]
                t = jnp.where(first, zero, t)
            elif blk == blocks - 1 and m >= win_tiles - pad_tiles:
                t = u_ref[0, pl.ds(pl.multiple_of(jnp.minimum(start, n - SUBLANES), SUBLANES), SUBLANES), :]
                t = jnp.where(last, zero, t)
            else:
                t = u_ref[0, pl.ds(pl.multiple_of(start, SUBLANES), SUBLANES), :]
            tiles.append(t)
        acc = [bias] * out_tiles
        for r in range(SUBLANES):
            sh = _shifted_tiles(tiles, r, win_tiles - 1, sub)
            for a in range(win_tiles - out_tiles):
                k = r + SUBLANES * a - (CONV_PAD - half)
                if 0 <= k < CONV_WIDTH:
                    w = w8_ref[k]
                    acc = [acc[j] + w * sh[j + a] for j in range(out_tiles)]
        done.extend(acc)
        if len(done) == CONV_NORM_TB // SUBLANES:
            conv = jnp.concatenate(done, axis=0)
            done = []
            mu = jnp.mean(conv, axis=-1, keepdims=True)
            cen = conv - mu
            var = jnp.mean(cen * cen, axis=-1, keepdims=True)
            y = cen * lax.rsqrt(var + EPS) * lg_ref[...] + lb_ref[...]
            row0 = CONV_TB * (blk + 1) - CONV_NORM_TB
            ya_ref[slot, row0:row0 + CONV_NORM_TB, :] = (y * _sigmoid(y)).astype(ya_ref.dtype)


def _conv_out_mlp_kernel(x_ref, u_ref, yb_ref, yc_ref, cw_ref, cb_ref, lg_ref, lb_ref, mod_ref, g2_ref, gf_ref,
                         wo_ref, w1_ref, w2_ref, o_ref, ya_ref, w8_ref, *, final_norm, n, tm):
    c = u_ref.shape[-1]
    tiles_per_seq = n // tm
    step = pl.program_id(0) * tiles_per_seq + pl.program_id(1)
    cur = step % 2
    for k in range(CONV_WIDTH):
        w8_ref[k] = jnp.broadcast_to(cw_ref[k:k + 1, :], (SUBLANES, c))
    sub = lax.broadcasted_iota(jnp.int32, (SUBLANES, c), 0)
    bias = jnp.broadcast_to(cb_ref[...], (SUBLANES, c))
    conv = functools.partial(_conformer_tile, u_ref, w8_ref, bias, lg_ref, lb_ref, ya_ref, n=n, tm=tm, sub=sub)

    @pl.when(step == 0)
    def _():
        conv(0, 0)

    o_ref[0] = _out_mlp_body(x_ref[0], ya_ref[cur], yb_ref[0], yc_ref[0], mod_ref, g2_ref, gf_ref,
                             wo_ref, w1_ref, w2_ref, final_norm)
    conv(1 - cur, (pl.program_id(1) + 1) % tiles_per_seq)


def _out_mlp(x, ya, yb, yc, mod, g2, gf, wo, w1, w2, *, final_norm):
    b, n, d = x.shape
    tm = min(n, 512)
    per_batch = mod.shape[0] != 1
    tok = lambda a: pl.BlockSpec((1, tm, a.shape[-1]), lambda i, j: (i, j, 0))
    resident = lambda a: pl.BlockSpec(a.shape, lambda i, j: (0, 0), pipeline_mode=pl.Buffered(1))
    return pl.pallas_call(
        functools.partial(_out_mlp_kernel, final_norm=final_norm),
        grid=(b, n // tm),
        in_specs=[tok(x), tok(ya), tok(yb), tok(yc),
                  pl.BlockSpec((1, 6, d), (lambda i, j: (i, 0, 0)) if per_batch else (lambda i, j: (0, 0, 0))),
                  pl.BlockSpec((1, d), lambda i, j: (0, 0)), pl.BlockSpec((1, d), lambda i, j: (0, 0)),
                  resident(wo), resident(w1), resident(w2)],
        out_specs=tok(x),
        out_shape=jax.ShapeDtypeStruct((b, n, d), F32),
        compiler_params=_params("parallel", "parallel"),
        name="out_proj_mlp",
    )(x, ya, yb, yc, mod, g2, gf, wo, w1, w2)


def _conv_out_mlp(x, u, yb, yc, conv_w, conv_b, ln_g, ln_b, mod, g2, gf, wo, w1, w2, *, final_norm):
    b, n, d = x.shape
    c = u.shape[-1]
    tm = min(n, 512)
    nt = n // tm
    assert tm % CONV_NORM_TB == 0 and CONV_NORM_TB % CONV_TB == 0
    tok = lambda a: pl.BlockSpec((1, tm, a.shape[-1]), lambda i, j: (i, j, 0))
    row = lambda width: pl.BlockSpec((1, width), lambda i, j: (0, 0))
    resident = lambda a: pl.BlockSpec(a.shape, lambda i, j: (0, 0), pipeline_mode=pl.Buffered(1))
    next_batch = lambda i, j: (jnp.minimum(i + (j + 1) // nt, b - 1), 0, 0)
    return pl.pallas_call(
        functools.partial(_conv_out_mlp_kernel, final_norm=final_norm, n=n, tm=tm),
        grid=(b, nt),
        in_specs=[tok(x), pl.BlockSpec((1, n, c), next_batch), tok(yb), tok(yc),
                  pl.BlockSpec(conv_w.shape, lambda i, j: (0, 0)), row(c), row(c), row(c),
                  pl.BlockSpec((1, 6, d), lambda i, j: (i, 0, 0)), row(d), row(d),
                  resident(wo), resident(w1), resident(w2)],
        out_specs=tok(x),
        out_shape=jax.ShapeDtypeStruct((b, n, d), F32),
        scratch_shapes=[pltpu.VMEM((2, tm, c), BF16), pltpu.VMEM((CONV_WIDTH, SUBLANES, c), F32)],
        compiler_params=_params("arbitrary", "arbitrary"),
        name="conv_out_proj_mlp",
    )(x, u, yb, yc, conv_w, conv_b.reshape(1, c), ln_g.reshape(1, c), ln_b.reshape(1, c), mod, g2, gf, wo, w1, w2)


def kernel(x, c, ctx, c_ctx, norm1_g, norm2_g, ada_w, ada_b, w_in, w_out, conv_w, conv_b, conv_ln_g, conv_ln_b,
           na_rpb, lru_conv_w, lru_conv_b, lru_wx, lru_bx, lru_wa, lru_ba, lru_lambda, mlp_w1, mlp_w2, final_g):
    depth = w_in.shape[0]
    bsz, _, d = x.shape
    conv_dim = conv_w.shape[-1]
    lru_dim = lru_conv_w.shape[-1]
    na_dim = NA_HEADS * NA_HEAD_DIM
    dims = dict(conv_dim=conv_dim, na_dim=na_dim, lru_dim=lru_dim)

    cond = jnp.concatenate([c, c_ctx[None, :]], axis=0)
    cond = jnp.pad(cond, ((0, -(bsz + 1) % SUBLANES), (0, 0)))
    mod_all = _modulation(cond, ada_w, ada_b)
    cx = ctx
    for l in range(depth):
        last = l == depth - 1
        mod = mod_all[l, :bsz].reshape(bsz, 6, d)
        mod_c = mod_all[l, bsz:bsz + 1].reshape(1, 6, d)
        w_in_l = w_in[l].astype(BF16)
        g1 = norm1_g[l].reshape(1, d)

        lat = _in_proj(x, mod, g1, w_in_l, **dims)
        cxp = _in_proj(cx, mod_c, g1, w_in_l, **dims, want=("k", "v", "rx", "rg") if last else IN_PROJ_OUTPUTS)

        y_b = _na_latent(lat["q"], lat["k"], lat["v"], cxp["k"], cxp["v"], na_rpb[l], unroll=32 if l % 2 == 0 else 16)
        y_c, yc_c = _rglru(lat["rx"], lat["rg"], cxp["rx"], cxp["rg"], lru_conv_w[l], lru_conv_b[l],
                           lru_wx[l], lru_bx[l], lru_wa[l], lru_ba[l], lru_lambda[l])

        w_out_l = w_out[l].astype(BF16)
        w1_l = mlp_w1[l].astype(BF16)
        w2_l = mlp_w2[l].astype(BF16)
        g2 = norm2_g[l].reshape(1, d)
        gf = final_g.reshape(1, d)
        x = _conv_out_mlp(x, lat["u"], y_b, y_c, conv_w[l], conv_b[l], conv_ln_g[l], conv_ln_b[l], mod, g2, gf,
                          w_out_l, w1_l, w2_l, final_norm=last)
        if not last:
            yc_a = _conformer(cxp["u"], conv_w[l], conv_b[l], conv_ln_g[l], conv_ln_b[l])
            yc_b = _na_context(cxp["q"], cxp["k"], cxp["v"])
            cx = _out_mlp(cx, yc_a, yc_b, yc_c, mod_c, g2, gf, w_out_l, w1_l, w2_l, final_norm=False)
    return x
```

```python
import functools

import jax
import jax.numpy as jnp
from jax import lax
from jax.experimental import pallas as pl
from jax.experimental.pallas import tpu as pltpu

GRID_W = 64
CONV_WIDTH = 31
NA_HEADS = 8
NA_HEAD_DIM = 64
NA_ROWS = 8
NA_COLS = 16
LRU_BLOCKS = 4
LRU_CONV = 4
LRU_C = 8.0
EPS = 1e-6
NEG_INF = -1e30

SUBLANES = 8
LANES = 128
HEADS_PER_STEP = 4
HEAD_LANES = HEADS_PER_STEP * NA_HEAD_DIM
VMEM_LIMIT_BYTES = 56 * 1024 * 1024

F32 = jnp.float32
BF16 = jnp.bfloat16


def _sigmoid(x):
    return 1.0 / (1.0 + jnp.exp(-x))


def _gelu_tanh(x):
    return 0.5 * x * (1.0 + jnp.tanh(0.7978845608028654 * (x + 0.044715 * (x * x * x))))


def _rms(x):
    return x * lax.rsqrt(jnp.mean(x * x, axis=-1, keepdims=True) + EPS)


def _layer_weight(w_all, layer, **kwargs):
    return pl.BlockSpec((None,) + w_all.shape[1:], lambda i, j: (layer, 0, 0), **kwargs)


def _params(*sem):
    return pltpu.CompilerParams(dimension_semantics=sem, vmem_limit_bytes=VMEM_LIMIT_BYTES)


def _mod_kernel(c_ref, w_ref, b_ref, o_ref):
    c = c_ref[...]
    s = c * _sigmoid(c)
    o_ref[0] = jnp.dot(s, w_ref[0], preferred_element_type=F32,
                       precision=lax.Precision.HIGHEST) + b_ref[0]


def _modulation(cc, ada_w, ada_b):
    depth, d, n = ada_w.shape
    r = cc.shape[0]
    tn = n // 4
    return pl.pallas_call(
        _mod_kernel,
        grid=(depth, n // tn),
        in_specs=[pl.BlockSpec((r, d), lambda l, j: (0, 0)),
                  pl.BlockSpec((1, d, tn), lambda l, j: (l, 0, j)),
                  pl.BlockSpec((1, 1, tn), lambda l, j: (l, 0, j))],
        out_specs=pl.BlockSpec((1, r, tn), lambda l, j: (l, 0, j)),
        out_shape=jax.ShapeDtypeStruct((depth, r, n), F32),
        compiler_params=_params("parallel", "parallel"),
        name="adaln_modulation",
    )(cc, ada_w, ada_b.reshape(depth, 1, n))


IN_PROJ_OUTPUTS = ("u", "q", "k", "v", "rx", "rg")


def _in_proj_kernel(x_ref, mod_ref, g_ref, w_ref, *out_refs, conv_dim, na_dim, lru_dim, want):
    out = dict(zip(want, out_refs))
    x = x_ref[0]
    h = _rms(x) * g_ref[...]
    h = h * (1.0 + mod_ref[0, 1:2, :]) + mod_ref[0, 0:1, :]
    hb = h.astype(BF16)

    def proj(off, width):
        return jnp.dot(hb, w_ref[:, off:off + width], preferred_element_type=F32)

    q_off = 2 * conv_dim
    if "u" in out:
        out["u"][0] = proj(0, conv_dim) * _sigmoid(proj(conv_dim, conv_dim))
    if "q" in out:
        out["q"][0] = (proj(q_off, na_dim) * (NA_HEAD_DIM ** -0.5)).astype(BF16)
    if "k" in out:
        out["k"][0] = proj(q_off + na_dim, na_dim).astype(BF16)
    if "v" in out:
        out["v"][0] = proj(q_off + 2 * na_dim, na_dim).astype(BF16)
    if "rx" in out:
        out["rx"][0] = proj(q_off + 3 * na_dim, lru_dim)
    if "rg" in out:
        out["rg"][0] = proj(q_off + 3 * na_dim + lru_dim, lru_dim)


IN_PROJ_TM = 1024


def _in_proj(x, mod, g, w_all, layer, *, conv_dim, na_dim, lru_dim, want=IN_PROJ_OUTPUTS):
    b, n, d = x.shape
    tm = min(n, IN_PROJ_TM)
    per_batch = mod.shape[0] != 1
    kinds = dict(u=(conv_dim, F32), q=(na_dim, BF16), k=(na_dim, BF16), v=(na_dim, BF16),
                 rx=(lru_dim, F32), rg=(lru_dim, F32))
    outs = pl.pallas_call(
        functools.partial(_in_proj_kernel, conv_dim=conv_dim, na_dim=na_dim, lru_dim=lru_dim, want=want),
        grid=(b, n // tm),
        in_specs=[pl.BlockSpec((1, tm, d), lambda i, j: (i, j, 0)),
                  pl.BlockSpec((1, 6, d), (lambda i, j: (i, 0, 0)) if per_batch else (lambda i, j: (0, 0, 0))),
                  pl.BlockSpec((1, d), lambda i, j: (0, 0)),
                  _layer_weight(w_all, layer)],
        out_specs=[pl.BlockSpec((1, tm, kinds[name][0]), lambda i, j: (i, j, 0)) for name in want],
        out_shape=[jax.ShapeDtypeStruct((b, n, kinds[name][0]), kinds[name][1]) for name in want],
        compiler_params=_params("parallel", "parallel"),
        name="in_proj",
    )(x, mod, g, w_all)
    return dict(zip(want, outs))


CONV_TB = 32
CONV_PAD = 16
CONV_NORM_TB = 128


def _shifted_tiles(tiles, off, count, sub):
    a, r = divmod(off, SUBLANES)
    if r == 0:
        return tiles[a:a + count]
    rolled = [pltpu.roll(t, SUBLANES - r, 0) for t in tiles[a:a + count + 1]]
    keep = sub < SUBLANES - r
    return [jnp.where(keep, rolled[i], rolled[i + 1]) for i in range(count)]


def _conformer_kernel(u_ref, w_ref, b_ref, lg_ref, lb_ref, o_ref, pad_ref, w8_ref, conv_ref, *, n):
    c = u_ref.shape[-1]
    zeros = jnp.zeros((CONV_PAD, c), F32)
    pad_ref[0:CONV_PAD, :] = zeros
    pad_ref[n + CONV_PAD:n + 2 * CONV_PAD, :] = zeros
    pad_ref[CONV_PAD:n + CONV_PAD, :] = u_ref[0]
    for k in range(CONV_WIDTH):
        w8_ref[k] = jnp.broadcast_to(w_ref[k:k + 1, :], (SUBLANES, c))
    sub = lax.broadcasted_iota(jnp.int32, (SUBLANES, c), 0)
    bias = jnp.broadcast_to(b_ref[...], (SUBLANES, c))
    half = CONV_WIDTH // 2
    out_tiles = CONV_TB // SUBLANES
    win_tiles = (CONV_TB + 2 * CONV_PAD) // SUBLANES

    def block(i, carry):
        start = pl.multiple_of(i * CONV_TB, CONV_TB)
        tiles = [pad_ref[pl.ds(start + SUBLANES * m, SUBLANES), :] for m in range(win_tiles)]
        acc = [bias] * out_tiles
        for r in range(SUBLANES):
            sh = _shifted_tiles(tiles, r, win_tiles - 1, sub)
            for a in range(win_tiles - out_tiles):
                k = r + SUBLANES * a - (CONV_PAD - half)
                if 0 <= k < CONV_WIDTH:
                    w = w8_ref[k]
                    acc = [acc[j] + w * sh[j + a] for j in range(out_tiles)]
        conv_ref[pl.ds(start, CONV_TB), :] = jnp.concatenate(acc, axis=0)
        return carry

    lax.fori_loop(0, n // CONV_TB, block, 0)

    def norm_block(i, carry):
        start = pl.multiple_of(i * CONV_NORM_TB, CONV_NORM_TB)
        conv = conv_ref[pl.ds(start, CONV_NORM_TB), :]
        mu = jnp.mean(conv, axis=-1, keepdims=True)
        cen = conv - mu
        var = jnp.mean(cen * cen, axis=-1, keepdims=True)
        y = cen * lax.rsqrt(var + EPS) * lg_ref[...] + lb_ref[...]
        o_ref[0, pl.ds(start, CONV_NORM_TB), :] = (y * _sigmoid(y)).astype(o_ref.dtype)
        return carry

    lax.fori_loop(0, n // CONV_NORM_TB, norm_block, 0, unroll=4)


def _conformer(u, w, b, ln_g, ln_b):
    bsz, n, c = u.shape
    row = pl.BlockSpec((1, c), lambda i: (0, 0))
    return pl.pallas_call(
        functools.partial(_conformer_kernel, n=n),
        grid=(bsz,),
        in_specs=[pl.BlockSpec((1, n, c), lambda i: (i, 0, 0)),
                  pl.BlockSpec(w.shape, lambda i: (0, 0)), row, row, row],
        out_specs=pl.BlockSpec((1, n, c), lambda i: (i, 0, 0)),
        out_shape=jax.ShapeDtypeStruct((bsz, n, c), BF16),
        scratch_shapes=[pltpu.VMEM((n + 2 * CONV_PAD, c), F32), pltpu.VMEM((CONV_WIDTH, SUBLANES, c), F32),
                        pltpu.VMEM((n, c), F32)],
        compiler_params=_params("parallel"),
        name="conformer_conv",
    )(u, w, b.reshape(1, c), ln_g.reshape(1, c), ln_b.reshape(1, c))


def _head_of_lane():
    return lax.broadcasted_iota(jnp.int32, (1, HEAD_LANES), 1) // NA_HEAD_DIM


def _stack_heads(qr, head):
    zero = jnp.zeros_like(qr)
    return jnp.concatenate([jnp.where(head == h, qr, zero) for h in range(HEADS_PER_STEP)], axis=0)


def _unstack_heads(o, head, m):
    out = o[0:m]
    for h in range(1, HEADS_PER_STEP):
        out = jnp.where(head == h, o[h * m:(h + 1) * m], out)
    return out


def _build_na_bias(rpb_ref, bias_ref, group):
    qi = lax.broadcasted_iota(jnp.int32, (GRID_W, GRID_W), 0)
    ki = lax.broadcasted_iota(jnp.int32, (GRID_W, GRID_W), 1)
    rel = ki - qi
    col_start = jnp.clip(qi - NA_COLS // 2, 0, GRID_W - NA_COLS)
    valid = (ki >= col_start) & (ki < col_start + NA_COLS)
    n_r = 2 * NA_ROWS - 1
    n_c = 2 * NA_COLS - 1

    def per_head(h, carry):
        q0 = pl.multiple_of(h * GRID_W, GRID_W)
        for ri in range(n_r):
            base = ((group * HEADS_PER_STEP + h) * n_r + ri) * n_c
            t = jnp.full((GRID_W, GRID_W), NEG_INF, F32)
            for dc in range(n_c):
                t = jnp.where(rel == dc - (NA_COLS - 1), rpb_ref[base + dc], t)
            t = jnp.where(valid, t, NEG_INF)
            for d in range(NA_ROWS):
                j = ri + d - (NA_ROWS - 1)
                if 0 <= j < NA_ROWS:
                    bias_ref[d, pl.ds(q0, GRID_W), j * GRID_W:(j + 1) * GRID_W] = t
        return carry

    lax.fori_loop(0, HEADS_PER_STEP, per_head, 0)


def _lane_reduce(tiles, combine, reduce):
    by_width = {}
    for t in tiles:
        w = t.shape[-1]
        by_width[w] = t if w not in by_width else combine(by_width[w], t)
    parts = [reduce(t, axis=-1, keepdims=True) for t in by_width.values()]
    out = parts[0]
    for part in parts[1:]:
        out = combine(out, part)
    return out


def _na_latent_kernel(rpb_ref, q_ref, k_ref, v_ref, kc_ref, vc_ref, o_ref, bias_ref, *, rows, unroll):
    @pl.when(pl.program_id(1) == 0)
    def _():
        _build_na_bias(rpb_ref, bias_ref, pl.program_id(0))

    head = _head_of_lane()
    band = NA_ROWS * GRID_W
    nt = (((1,), (1,)), ((), ()))
    bounds = list(range(0, band, HEAD_LANES)) + [band]

    def row_step(r, carry):
        rs = jnp.clip(r - NA_ROWS // 2, 0, rows - NA_ROWS)
        q0 = pl.multiple_of(r * GRID_W, GRID_W)
        k0 = pl.multiple_of(rs * GRID_W, GRID_W)
        qs = _stack_heads(q_ref[0, pl.ds(q0, GRID_W), :], head)
        s_loc = lax.dot_general(qs, k_ref[0, pl.ds(k0, band), :], nt, preferred_element_type=F32) + bias_ref[r - rs]
        s_ctx = lax.dot_general(qs, kc_ref[0], nt, preferred_element_type=F32)
        tiles = [s_loc[:, lo:hi] for lo, hi in zip(bounds[:-1], bounds[1:])] + [s_ctx]
        m = _lane_reduce(tiles, jnp.maximum, jnp.max)
        probs = [jnp.exp(t - m) for t in tiles]
        inv = 1.0 / _lane_reduce(probs, jnp.add, jnp.sum)
        o = jnp.dot(jnp.concatenate(probs[:-1], axis=-1).astype(BF16), v_ref[0, pl.ds(k0, band), :],
                    preferred_element_type=F32)
        o = (o + jnp.dot(probs[-1].astype(BF16), vc_ref[0], preferred_element_type=F32)) * inv
        o_ref[0, pl.ds(q0, GRID_W), :] = _unstack_heads(o, head, GRID_W).astype(o_ref.dtype)
        return carry

    lax.fori_loop(0, rows, row_step, 0, unroll=unroll)


NA_ROW_UNROLL = 32


def _na_latent(q, k, v, kc, vc, rpb, *, unroll):
    bsz, n, na_dim = q.shape
    nc = kc.shape[1]
    groups = na_dim // HEAD_LANES
    rows = n // GRID_W
    unroll = min(unroll, rows)
    assert rows >= NA_ROWS and rows % unroll == 0
    lat = pl.BlockSpec((1, n, HEAD_LANES), lambda g, i: (i, 0, g))
    ctx = pl.BlockSpec((1, nc, HEAD_LANES), lambda g, i: (i, 0, g))
    return pl.pallas_call(
        functools.partial(_na_latent_kernel, rows=rows, unroll=unroll),
        grid=(groups, bsz),
        in_specs=[pl.BlockSpec(memory_space=pltpu.SMEM), lat, lat, lat, ctx, ctx],
        out_specs=lat,
        out_shape=jax.ShapeDtypeStruct((bsz, n, na_dim), BF16),
        scratch_shapes=[pltpu.VMEM((NA_ROWS, HEADS_PER_STEP * GRID_W, NA_ROWS * GRID_W), F32)],
        compiler_params=_params("arbitrary", "arbitrary"),
        name="na_latent",
    )(rpb.reshape(-1), q, k, v, kc, vc)


def _na_context_kernel(q_ref, k_ref, v_ref, o_ref):
    head = _head_of_lane()
    n = q_ref.shape[1]
    qs = _stack_heads(q_ref[0], head)
    s = lax.dot_general(qs, k_ref[0], (((1,), (1,)), ((), ())), preferred_element_type=F32)
    p = jnp.exp(s - jnp.max(s, axis=-1, keepdims=True))
    denom = jnp.sum(p, axis=-1, keepdims=True)
    o = jnp.dot(p.astype(BF16), v_ref[0], preferred_element_type=F32) * (1.0 / denom)
    o_ref[0] = _unstack_heads(o, head, n).astype(o_ref.dtype)


def _na_context(q, k, v):
    bsz, n, na_dim = q.shape
    blk = pl.BlockSpec((1, n, HEAD_LANES), lambda i, g: (i, 0, g))
    return pl.pallas_call(
        _na_context_kernel,
        grid=(bsz, na_dim // HEAD_LANES),
        in_specs=[blk, blk, blk],
        out_specs=blk,
        out_shape=jax.ShapeDtypeStruct((bsz, n, na_dim), BF16),
        compiler_params=_params("parallel", "parallel"),
        name="na_context",
    )(q, k, v)


LRU_TB = 256
LRU_PAD = 8


def _rglru_kernel(xl_ref, gl_ref, xc_ref, gc_ref, cw_ref, vec_ref, wx_ref, wa_ref, yl_ref, yc_ref,
                  padl_ref, padc_ref, a_ref, b_ref, h_ref, *, n_lat, n_ctx):
    c = xl_ref.shape[-1]
    zeros = jnp.zeros((LRU_PAD, c), F32)
    for pad_ref, src_ref, n in ((padc_ref, xc_ref, n_ctx), (padl_ref, xl_ref, n_lat)):
        pad_ref[0:LRU_PAD, :] = zeros
        pad_ref[n + LRU_PAD:n + 2 * LRU_PAD, :] = zeros
        pad_ref[LRU_PAD:n + LRU_PAD, :] = src_ref[0]
    sub = lax.broadcasted_iota(jnp.int32, (SUBLANES, c), 0)

    for direction in range(2):
        reverse = direction == 1
        conv_b = vec_ref[direction, 0:1, :]
        bx = vec_ref[direction, 1:2, :]
        ba = vec_ref[direction, 2:3, :]
        neg_lam = -vec_ref[direction, 3:4, :]
        softplus = jnp.maximum(neg_lam, 0.0) + jnp.log(1.0 + jnp.exp(-jnp.abs(neg_lam)))
        decay = -LRU_C * softplus

        conv_b8 = jnp.broadcast_to(conv_b, (SUBLANES, c))
        cw8 = [jnp.broadcast_to(cw_ref[direction, k:k + 1, :], (SUBLANES, c)) for k in range(LRU_CONV)]

        def coeffs(pad_ref, base, n):
            tb = min(n, LRU_TB)
            nt = tb // SUBLANES

            def block(i, carry):
                start = pl.multiple_of(i * tb, tb)
                tiles = [pad_ref[pl.ds(start + SUBLANES * m, SUBLANES), :]
                         for m in range(nt + 2 * LRU_PAD // SUBLANES)]
                u_tiles = [conv_b8] * nt
                for k in range(LRU_CONV):
                    off = LRU_PAD + k - (0 if reverse else LRU_CONV - 1)
                    sh = _shifted_tiles(tiles, off, nt, sub)
                    u_tiles = [u_tiles[j] + cw8[k] * sh[j] for j in range(nt)]
                u = jnp.concatenate(u_tiles, axis=0)
                ub = u.astype(BF16)
                gx = 1.0 / (1.0 + jnp.exp(jnp.dot(ub, wx_ref[direction], preferred_element_type=F32) - bx))
                ga = 1.0 / (1.0 + jnp.exp(jnp.dot(ub, wa_ref[direction], preferred_element_type=F32) - ba))
                a = jnp.exp(decay * ga)
                rem = 1.0 - a * a
                coef = jnp.where(rem > 0.0, rem * lax.rsqrt(rem), 0.0)
                a_ref[pl.ds(base + start, tb), :] = a
                b_ref[pl.ds(base + start, tb), :] = coef * (gx * u)
                return carry

            lax.fori_loop(0, n // tb, block, 0)

        coeffs(padc_ref, 0, n_ctx)
        coeffs(padl_ref, n_ctx, n_lat)

        def scan(base, n, hb):
            chunks = n // SUBLANES

            def chunk(j, hb):
                jj = chunks - 1 - j if reverse else j
                rows = pl.ds(pl.multiple_of(base + jj * SUBLANES, SUBLANES), SUBLANES)
                a = a_ref[rows, :]
                b = b_ref[rows, :]
                for d in (1, 2, 4):
                    shift = SUBLANES - d if reverse else d
                    keep = (sub < SUBLANES - d) if reverse else (sub >= d)
                    ra = pltpu.roll(a, shift, 0)
                    rb = pltpu.roll(b, shift, 0)
                    b = jnp.where(keep, a * rb + b, b)
                    a = jnp.where(keep, a * ra, a)
                h = a * hb + b
                if reverse:
                    h_ref[rows, :] = h_ref[rows, :] + h
                else:
                    h_ref[rows, :] = h
                last = 0 if reverse else SUBLANES - 1
                a_last = jnp.broadcast_to(a[last:last + 1, :], (SUBLANES, c))
                b_last = jnp.broadcast_to(b[last:last + 1, :], (SUBLANES, c))
                return a_last * hb + b_last

            return lax.fori_loop(0, chunks, chunk, hb, unroll=8)

        hb = scan(0, n_ctx, jnp.zeros((SUBLANES, c), F32))
        scan(n_ctx, n_lat, hb)

    def gate_out(g_ref, y_ref, base, n):
        tb = min(n, LRU_TB)

        def block(i, carry):
            start = pl.multiple_of(i * tb, tb)
            g = g_ref[0, pl.ds(start, tb), :]
            y_ref[0, pl.ds(start, tb), :] = (_gelu_tanh(g) * h_ref[pl.ds(base + start, tb), :]).astype(y_ref.dtype)
            return carry

        lax.fori_loop(0, n // tb, block, 0)

    gate_out(gc_ref, yc_ref, 0, n_ctx)
    gate_out(gl_ref, yl_ref, n_ctx, n_lat)


def _block_diag(w):
    two, nb, m, _ = w.shape
    eye = jnp.eye(nb, dtype=w.dtype)
    return (w[:, :, :, None, :] * eye[None, :, None, :, None]).reshape(two, nb * m, nb * m)


def _rglru(xl, gl, xc, gc, conv_w, conv_b, wx, bx, wa, ba, lam):
    bsz, n_lat, c = xl.shape
    n_ctx = xc.shape[1]
    vec = jnp.stack([conv_b, bx, ba, lam], axis=1)
    wxd = _block_diag(-wx).astype(BF16)
    wad = _block_diag(-wa).astype(BF16)
    lat = pl.BlockSpec((1, n_lat, c), lambda i: (i, 0, 0))
    ctx = pl.BlockSpec((1, n_ctx, c), lambda i: (i, 0, 0))
    full = lambda a: pl.BlockSpec(a.shape, lambda i: (0,) * a.ndim)
    return pl.pallas_call(
        functools.partial(_rglru_kernel, n_lat=n_lat, n_ctx=n_ctx),
        grid=(bsz,),
        in_specs=[lat, lat, ctx, ctx, full(conv_w), full(vec), full(wxd), full(wad)],
        out_specs=[lat, ctx],
        out_shape=[jax.ShapeDtypeStruct((bsz, n_lat, c), BF16), jax.ShapeDtypeStruct((bsz, n_ctx, c), BF16)],
        scratch_shapes=[pltpu.VMEM((n_lat + 2 * LRU_PAD, c), F32), pltpu.VMEM((n_ctx + 2 * LRU_PAD, c), F32),
                        pltpu.VMEM((n_ctx + n_lat, c), F32), pltpu.VMEM((n_ctx + n_lat, c), F32),
                        pltpu.VMEM((n_ctx + n_lat, c), F32)],
        compiler_params=_params("parallel"),
        name="rglru",
    )(xl, gl, xc, gc, conv_w, vec, wxd, wad)


FF_CHUNK = 1024


def _out_mlp_body(x, ya, yb, yc, mod_ref, g2_ref, gf_ref, wo_ref, w1_ref, w2_ref, final_norm):
    ca, cb, cc = ya.shape[-1], yb.shape[-1], yc.shape[-1]
    mix = (jnp.dot(ya, wo_ref[0:ca, :], preferred_element_type=F32)
           + jnp.dot(yb, wo_ref[ca:ca + cb, :], preferred_element_type=F32)
           + jnp.dot(yc, wo_ref[ca + cb:ca + cb + cc, :], preferred_element_type=F32))
    x1 = x + mod_ref[0, 2:3, :] * mix
    h2 = _rms(x1) * g2_ref[...]
    h2 = (h2 * (1.0 + mod_ref[0, 4:5, :]) + mod_ref[0, 3:4, :]).astype(BF16)
    acc = jnp.zeros(x1.shape, F32)
    for j in range(w1_ref.shape[1] // FF_CHUNK):
        cols = slice(j * FF_CHUNK, (j + 1) * FF_CHUNK)
        hid = jnp.maximum(jnp.dot(h2, w1_ref[:, cols], preferred_element_type=F32), 0.0)
        acc = acc + jnp.dot((hid * hid).astype(BF16), w2_ref[cols, :], preferred_element_type=F32)
    out = x1 + mod_ref[0, 5:6, :] * acc
    if final_norm:
        out = _rms(out) * gf_ref[...]
    return out


def _out_mlp_kernel(x_ref, ya_ref, yb_ref, yc_ref, mod_ref, g2_ref, gf_ref, wo_ref, w1_ref, w2_ref, o_ref, *,
                    final_norm):
    o_ref[0] = _out_mlp_body(x_ref[0], ya_ref[0], yb_ref[0], yc_ref[0], mod_ref, g2_ref, gf_ref,
                             wo_ref, w1_ref, w2_ref, final_norm)


def _conformer_tile(u_ref, w8_ref, bias, lg_ref, lb_ref, ya_ref, slot, tile, *, n, tm, sub):
    half = CONV_WIDTH // 2
    out_tiles = CONV_TB // SUBLANES
    win_tiles = (CONV_TB + 2 * CONV_PAD) // SUBLANES
    pad_tiles = CONV_PAD // SUBLANES
    blocks = tm // CONV_TB
    first = tile == 0
    last = tile == n // tm - 1
    zero = jnp.zeros((SUBLANES, u_ref.shape[-1]), F32)
    done = []
    for blk in range(blocks):
        base = tile * tm + (CONV_TB * blk - CONV_PAD)
        tiles = []
        for m in range(win_tiles):
            start = base + SUBLANES * m
            if blk == 0 and m < pad_tiles:
                t = u_ref[0, pl.ds(pl.multiple_of(jnp.maximum(start, 0), SUBLANES), SUBLANES), :]
                t = jnp.where(first, zero, t)
            elif blk == blocks - 1 and m >= win_tiles - pad_tiles:
                t = u_ref[0, pl.ds(pl.multiple_of(jnp.minimum(start, n - SUBLANES), SUBLANES), SUBLANES), :]
                t = jnp.where(last, zero, t)
            else:
                t = u_ref[0, pl.ds(pl.multiple_of(start, SUBLANES), SUBLANES), :]
            tiles.append(t)
        acc = [bias] * out_tiles
        for r in range(SUBLANES):
            sh = _shifted_tiles(tiles, r, win_tiles - 1, sub)
            for a in range(win_tiles - out_tiles):
                k = r + SUBLANES * a - (CONV_PAD - half)
                if 0 <= k < CONV_WIDTH:
                    w = w8_ref[k]
                    acc = [acc[j] + w * sh[j + a] for j in range(out_tiles)]
        done.extend(acc)
        if len(done) == CONV_NORM_TB // SUBLANES:
            conv = jnp.concatenate(done, axis=0)
            done = []
            mu = jnp.mean(conv, axis=-1, keepdims=True)
            cen = conv - mu
            var = jnp.mean(cen * cen, axis=-1, keepdims=True)
            y = cen * lax.rsqrt(var + EPS) * lg_ref[...] + lb_ref[...]
            row0 = CONV_TB * (blk + 1) - CONV_NORM_TB
            ya_ref[slot, row0:row0 + CONV_NORM_TB, :] = (y * _sigmoid(y)).astype(ya_ref.dtype)


def _conv_out_mlp_kernel(x_ref, u_ref, yb_ref, yc_ref, cw_ref, cb_ref, lg_ref, lb_ref, mod_ref, g2_ref, gf_ref,
                         wo_ref, w1_ref, w2_ref, o_ref, ya_ref, w8_ref, *, final_norm, n, tm):
    c = u_ref.shape[-1]
    tiles_per_seq = n // tm
    step = pl.program_id(0) * tiles_per_seq + pl.program_id(1)
    cur = step % 2
    for k in range(CONV_WIDTH):
        w8_ref[k] = jnp.broadcast_to(cw_ref[k:k + 1, :], (SUBLANES, c))
    sub = lax.broadcasted_iota(jnp.int32, (SUBLANES, c), 0)
    bias = jnp.broadcast_to(cb_ref[...], (SUBLANES, c))
    conv = functools.partial(_conformer_tile, u_ref, w8_ref, bias, lg_ref, lb_ref, ya_ref, n=n, tm=tm, sub=sub)

    @pl.when(step == 0)
    def _():
        conv(0, 0)

    o_ref[0] = _out_mlp_body(x_ref[0], ya_ref[cur], yb_ref[0], yc_ref[0], mod_ref, g2_ref, gf_ref,
                             wo_ref, w1_ref, w2_ref, final_norm)
    conv(1 - cur, (pl.program_id(1) + 1) % tiles_per_seq)


def _out_mlp(x, ya, yb, yc, mod, g2, gf, wo, w1, w2, layer, *, final_norm):
    b, n, d = x.shape
    tm = min(n, 512)
    per_batch = mod.shape[0] != 1
    tok = lambda a: pl.BlockSpec((1, tm, a.shape[-1]), lambda i, j: (i, j, 0))
    resident = lambda a: _layer_weight(a, layer, pipeline_mode=pl.Buffered(1))
    return pl.pallas_call(
        functools.partial(_out_mlp_kernel, final_norm=final_norm),
        grid=(b, n // tm),
        in_specs=[tok(x), tok(ya), tok(yb), tok(yc),
                  pl.BlockSpec((1, 6, d), (lambda i, j: (i, 0, 0)) if per_batch else (lambda i, j: (0, 0, 0))),
                  pl.BlockSpec((1, d), lambda i, j: (0, 0)), pl.BlockSpec((1, d), lambda i, j: (0, 0)),
                  resident(wo), resident(w1), resident(w2)],
        out_specs=tok(x),
        out_shape=jax.ShapeDtypeStruct((b, n, d), F32),
        compiler_params=_params("parallel", "parallel"),
        name="out_proj_mlp",
    )(x, ya, yb, yc, mod, g2, gf, wo, w1, w2)


def _conv_out_mlp(x, u, yb, yc, conv_w, conv_b, ln_g, ln_b, mod, g2, gf, wo, w1, w2, layer, *, final_norm):
    b, n, d = x.shape
    c = u.shape[-1]
    tm = min(n, 512)
    nt = n // tm
    assert tm % CONV_NORM_TB == 0 and CONV_NORM_TB % CONV_TB == 0
    tok = lambda a: pl.BlockSpec((1, tm, a.shape[-1]), lambda i, j: (i, j, 0))
    row = lambda width: pl.BlockSpec((1, width), lambda i, j: (0, 0))
    resident = lambda a: _layer_weight(a, layer, pipeline_mode=pl.Buffered(1))
    next_batch = lambda i, j: (jnp.minimum(i + (j + 1) // nt, b - 1), 0, 0)
    return pl.pallas_call(
        functools.partial(_conv_out_mlp_kernel, final_norm=final_norm, n=n, tm=tm),
        grid=(b, nt),
        in_specs=[tok(x), pl.BlockSpec((1, n, c), next_batch), tok(yb), tok(yc),
                  pl.BlockSpec(conv_w.shape, lambda i, j: (0, 0)), row(c), row(c), row(c),
                  pl.BlockSpec((1, 6, d), lambda i, j: (i, 0, 0)), row(d), row(d),
                  resident(wo), resident(w1), resident(w2)],
        out_specs=tok(x),
        out_shape=jax.ShapeDtypeStruct((b, n, d), F32),
        scratch_shapes=[pltpu.VMEM((2, tm, c), BF16), pltpu.VMEM((CONV_WIDTH, SUBLANES, c), F32)],
        compiler_params=_params("arbitrary", "arbitrary"),
        name="conv_out_proj_mlp",
    )(x, u, yb, yc, conv_w, conv_b.reshape(1, c), ln_g.reshape(1, c), ln_b.reshape(1, c), mod, g2, gf, wo, w1, w2)


def kernel(x, c, ctx, c_ctx, norm1_g, norm2_g, ada_w, ada_b, w_in, w_out, conv_w, conv_b, conv_ln_g, conv_ln_b,
           na_rpb, lru_conv_w, lru_conv_b, lru_wx, lru_bx, lru_wa, lru_ba, lru_lambda, mlp_w1, mlp_w2, final_g):
    depth = w_in.shape[0]
    bsz, _, d = x.shape
    conv_dim = conv_w.shape[-1]
    lru_dim = lru_conv_w.shape[-1]
    na_dim = NA_HEADS * NA_HEAD_DIM
    dims = dict(conv_dim=conv_dim, na_dim=na_dim, lru_dim=lru_dim)

    cond = jnp.concatenate([c, c_ctx[None, :]], axis=0)
    cond = jnp.pad(cond, ((0, -(bsz + 1) % SUBLANES), (0, 0)))
    mod_all = _modulation(cond, ada_w, ada_b)
    w_in_b, w_out_b = w_in.astype(BF16), w_out.astype(BF16)
    w1_b, w2_b = mlp_w1.astype(BF16), mlp_w2.astype(BF16)
    cx = ctx
    for l in range(depth):
        last = l == depth - 1
        mod = mod_all[l, :bsz].reshape(bsz, 6, d)
        mod_c = mod_all[l, bsz:bsz + 1].reshape(1, 6, d)
        g1 = norm1_g[l].reshape(1, d)

        lat = _in_proj(x, mod, g1, w_in_b, l, **dims)
        cxp = _in_proj(cx, mod_c, g1, w_in_b, l, **dims, want=("k", "v", "rx", "rg") if last else IN_PROJ_OUTPUTS)

        y_b = _na_latent(lat["q"], lat["k"], lat["v"], cxp["k"], cxp["v"], na_rpb[l], unroll=NA_ROW_UNROLL)
        y_c, yc_c = _rglru(lat["rx"], lat["rg"], cxp["rx"], cxp["rg"], lru_conv_w[l], lru_conv_b[l],
                           lru_wx[l], lru_bx[l], lru_wa[l], lru_ba[l], lru_lambda[l])

        g2 = norm2_g[l].reshape(1, d)
        gf = final_g.reshape(1, d)
        x = _conv_out_mlp(x, lat["u"], y_b, y_c, conv_w[l], conv_b[l], conv_ln_g[l], conv_ln_b[l], mod, g2, gf,
                          w_out_b, w1_b, w2_b, l, final_norm=last)
        if not last:
            yc_a = _conformer(cxp["u"], conv_w[l], conv_b[l], conv_ln_g[l], conv_ln_b[l])
            yc_b = _na_context(cxp["q"], cxp["k"], cxp["v"])
            cx = _out_mlp(cx, yc_a, yc_b, yc_c, mod_c, g2, gf, w_out_b, w1_b, w2_b, l, final_norm=False)
    return x
```

```python
import functools

import jax
import jax.numpy as jnp
from jax import lax
from jax.experimental import pallas as pl
from jax.experimental.pallas import tpu as pltpu

GRID_W = 64
CONV_WIDTH = 31
NA_HEADS = 8
NA_HEAD_DIM = 64
NA_ROWS = 8
NA_COLS = 16
LRU_BLOCKS = 4
LRU_CONV = 4
LRU_C = 8.0
EPS = 1e-6
NEG_INF = -1e30
LOG2_E = 1.4426950408889634

SUBLANES = 8
LANES = 128
HEADS_PER_STEP = 4
HEAD_LANES = HEADS_PER_STEP * NA_HEAD_DIM
VMEM_LIMIT_BYTES = 56 * 1024 * 1024

F32 = jnp.float32
BF16 = jnp.bfloat16


def _sigmoid(x):
    return 1.0 / (1.0 + jnp.exp(-x))


def _gelu_tanh(x):
    return 0.5 * x * (1.0 + jnp.tanh(0.7978845608028654 * (x + 0.044715 * (x * x * x))))


def _rms(x):
    return x * lax.rsqrt(jnp.mean(x * x, axis=-1, keepdims=True) + EPS)


def _layer_weight(w_all, layer, **kwargs):
    return pl.BlockSpec((None,) + w_all.shape[1:], lambda i, j: (layer, 0, 0), **kwargs)


def _params(*sem):
    return pltpu.CompilerParams(dimension_semantics=sem, vmem_limit_bytes=VMEM_LIMIT_BYTES)


def _mod_kernel(c_ref, w_ref, b_ref, o_ref):
    c = c_ref[...]
    s = c * _sigmoid(c)
    o_ref[0] = jnp.dot(s, w_ref[0], preferred_element_type=F32,
                       precision=lax.Precision.HIGHEST) + b_ref[0]


def _modulation(cc, ada_w, ada_b):
    depth, d, n = ada_w.shape
    r = cc.shape[0]
    tn = n // 4
    return pl.pallas_call(
        _mod_kernel,
        grid=(depth, n // tn),
        in_specs=[pl.BlockSpec((r, d), lambda l, j: (0, 0)),
                  pl.BlockSpec((1, d, tn), lambda l, j: (l, 0, j)),
                  pl.BlockSpec((1, 1, tn), lambda l, j: (l, 0, j))],
        out_specs=pl.BlockSpec((1, r, tn), lambda l, j: (l, 0, j)),
        out_shape=jax.ShapeDtypeStruct((depth, r, n), F32),
        compiler_params=_params("parallel", "parallel"),
        name="adaln_modulation",
    )(cc, ada_w, ada_b.reshape(depth, 1, n))


IN_PROJ_OUTPUTS = ("u", "q", "k", "v", "rx", "rg")


def _in_proj_kernel(x_ref, mod_ref, g_ref, w_ref, *out_refs, conv_dim, na_dim, lru_dim, want):
    out = dict(zip(want, out_refs))
    x = x_ref[0]
    h = _rms(x) * g_ref[...]
    h = h * (1.0 + mod_ref[0, 1:2, :]) + mod_ref[0, 0:1, :]
    hb = h.astype(BF16)

    def proj(off, width):
        return jnp.dot(hb, w_ref[:, off:off + width], preferred_element_type=F32)

    q_off = 2 * conv_dim
    if "u" in out:
        out["u"][0] = proj(0, conv_dim) * _sigmoid(proj(conv_dim, conv_dim))
    if "q" in out:
        out["q"][0] = (proj(q_off, na_dim) * (NA_HEAD_DIM ** -0.5 * LOG2_E)).astype(BF16)
    if "k" in out:
        out["k"][0] = proj(q_off + na_dim, na_dim).astype(BF16)
    if "v" in out:
        out["v"][0] = proj(q_off + 2 * na_dim, na_dim).astype(BF16)
    if "rx" in out:
        out["rx"][0] = proj(q_off + 3 * na_dim, lru_dim)
    if "rg" in out:
        out["rg"][0] = proj(q_off + 3 * na_dim + lru_dim, lru_dim)


IN_PROJ_TM = 1024


def _in_proj(x, mod, g, w_all, layer, *, conv_dim, na_dim, lru_dim, want=IN_PROJ_OUTPUTS):
    b, n, d = x.shape
    tm = min(n, IN_PROJ_TM)
    per_batch = mod.shape[0] != 1
    kinds = dict(u=(conv_dim, F32), q=(na_dim, BF16), k=(na_dim, BF16), v=(na_dim, BF16),
                 rx=(lru_dim, F32), rg=(lru_dim, F32))
    outs = pl.pallas_call(
        functools.partial(_in_proj_kernel, conv_dim=conv_dim, na_dim=na_dim, lru_dim=lru_dim, want=want),
        grid=(b, n // tm),
        in_specs=[pl.BlockSpec((1, tm, d), lambda i, j: (i, j, 0)),
                  pl.BlockSpec((1, 6, d), (lambda i, j: (i, 0, 0)) if per_batch else (lambda i, j: (0, 0, 0))),
                  pl.BlockSpec((1, d), lambda i, j: (0, 0)),
                  _layer_weight(w_all, layer)],
        out_specs=[pl.BlockSpec((1, tm, kinds[name][0]), lambda i, j: (i, j, 0)) for name in want],
        out_shape=[jax.ShapeDtypeStruct((b, n, kinds[name][0]), kinds[name][1]) for name in want],
        compiler_params=_params("parallel", "parallel"),
        name="in_proj",
    )(x, mod, g, w_all)
    return dict(zip(want, outs))


CONV_TB = 32
CONV_PAD = 16
CONV_NORM_TB = 128


def _shifted_tiles(tiles, off, count, sub):
    a, r = divmod(off, SUBLANES)
    if r == 0:
        return tiles[a:a + count]
    rolled = [pltpu.roll(t, SUBLANES - r, 0) for t in tiles[a:a + count + 1]]
    keep = sub < SUBLANES - r
    return [jnp.where(keep, rolled[i], rolled[i + 1]) for i in range(count)]


def _conformer_kernel(u_ref, w_ref, b_ref, lg_ref, lb_ref, o_ref, pad_ref, w8_ref, conv_ref, *, n):
    c = u_ref.shape[-1]
    zeros = jnp.zeros((CONV_PAD, c), F32)
    pad_ref[0:CONV_PAD, :] = zeros
    pad_ref[n + CONV_PAD:n + 2 * CONV_PAD, :] = zeros
    pad_ref[CONV_PAD:n + CONV_PAD, :] = u_ref[0]
    for k in range(CONV_WIDTH):
        w8_ref[k] = jnp.broadcast_to(w_ref[k:k + 1, :], (SUBLANES, c))
    sub = lax.broadcasted_iota(jnp.int32, (SUBLANES, c), 0)
    bias = jnp.broadcast_to(b_ref[...], (SUBLANES, c))
    half = CONV_WIDTH // 2
    out_tiles = CONV_TB // SUBLANES
    win_tiles = (CONV_TB + 2 * CONV_PAD) // SUBLANES

    def block(i, carry):
        start = pl.multiple_of(i * CONV_TB, CONV_TB)
        tiles = [pad_ref[pl.ds(start + SUBLANES * m, SUBLANES), :] for m in range(win_tiles)]
        acc = [bias] * out_tiles
        for r in range(SUBLANES):
            sh = _shifted_tiles(tiles, r, win_tiles - 1, sub)
            for a in range(win_tiles - out_tiles):
                k = r + SUBLANES * a - (CONV_PAD - half)
                if 0 <= k < CONV_WIDTH:
                    w = w8_ref[k]
                    acc = [acc[j] + w * sh[j + a] for j in range(out_tiles)]
        conv_ref[pl.ds(start, CONV_TB), :] = jnp.concatenate(acc, axis=0)
        return carry

    lax.fori_loop(0, n // CONV_TB, block, 0)

    def norm_block(i, carry):
        start = pl.multiple_of(i * CONV_NORM_TB, CONV_NORM_TB)
        conv = conv_ref[pl.ds(start, CONV_NORM_TB), :]
        mu = jnp.mean(conv, axis=-1, keepdims=True)
        cen = conv - mu
        var = jnp.mean(cen * cen, axis=-1, keepdims=True)
        y = cen * lax.rsqrt(var + EPS) * lg_ref[...] + lb_ref[...]
        o_ref[0, pl.ds(start, CONV_NORM_TB), :] = (y * _sigmoid(y)).astype(o_ref.dtype)
        return carry

    lax.fori_loop(0, n // CONV_NORM_TB, norm_block, 0, unroll=4)


def _conformer(u, w, b, ln_g, ln_b):
    bsz, n, c = u.shape
    row = pl.BlockSpec((1, c), lambda i: (0, 0))
    return pl.pallas_call(
        functools.partial(_conformer_kernel, n=n),
        grid=(bsz,),
        in_specs=[pl.BlockSpec((1, n, c), lambda i: (i, 0, 0)),
                  pl.BlockSpec(w.shape, lambda i: (0, 0)), row, row, row],
        out_specs=pl.BlockSpec((1, n, c), lambda i: (i, 0, 0)),
        out_shape=jax.ShapeDtypeStruct((bsz, n, c), BF16),
        scratch_shapes=[pltpu.VMEM((n + 2 * CONV_PAD, c), F32), pltpu.VMEM((CONV_WIDTH, SUBLANES, c), F32),
                        pltpu.VMEM((n, c), F32)],
        compiler_params=_params("parallel"),
        name="conformer_conv",
    )(u, w, b.reshape(1, c), ln_g.reshape(1, c), ln_b.reshape(1, c))


def _head_of_lane():
    return lax.broadcasted_iota(jnp.int32, (1, HEAD_LANES), 1) // NA_HEAD_DIM


def _stack_heads(qr, head):
    zero = jnp.zeros_like(qr)
    return jnp.concatenate([jnp.where(head == h, qr, zero) for h in range(HEADS_PER_STEP)], axis=0)


def _unstack_heads(o, head, m):
    out = o[0:m]
    for h in range(1, HEADS_PER_STEP):
        out = jnp.where(head == h, o[h * m:(h + 1) * m], out)
    return out


def _build_na_bias(rpb_ref, bias_ref, group):
    qi = lax.broadcasted_iota(jnp.int32, (GRID_W, GRID_W), 0)
    ki = lax.broadcasted_iota(jnp.int32, (GRID_W, GRID_W), 1)
    rel = ki - qi
    col_start = jnp.clip(qi - NA_COLS // 2, 0, GRID_W - NA_COLS)
    valid = (ki >= col_start) & (ki < col_start + NA_COLS)
    n_r = 2 * NA_ROWS - 1
    n_c = 2 * NA_COLS - 1

    def per_head(h, carry):
        q0 = pl.multiple_of(h * GRID_W, GRID_W)
        for ri in range(n_r):
            base = ((group * HEADS_PER_STEP + h) * n_r + ri) * n_c
            t = jnp.full((GRID_W, GRID_W), NEG_INF, F32)
            for dc in range(n_c):
                t = jnp.where(rel == dc - (NA_COLS - 1), rpb_ref[base + dc] * LOG2_E, t)
            t = jnp.where(valid, t, NEG_INF)
            for d in range(NA_ROWS):
                j = ri + d - (NA_ROWS - 1)
                if 0 <= j < NA_ROWS:
                    bias_ref[d, pl.ds(q0, GRID_W), j * GRID_W:(j + 1) * GRID_W] = t
        return carry

    lax.fori_loop(0, HEADS_PER_STEP, per_head, 0)


def _lane_reduce(tiles, combine, reduce):
    by_width = {}
    for t in tiles:
        w = t.shape[-1]
        by_width[w] = t if w not in by_width else combine(by_width[w], t)
    parts = [reduce(t, axis=-1, keepdims=True) for t in by_width.values()]
    out = parts[0]
    for part in parts[1:]:
        out = combine(out, part)
    return out


def _na_latent_kernel(rpb_ref, q_ref, k_ref, v_ref, kc_ref, vc_ref, o_ref, bias_ref, *, rows, unroll):
    @pl.when(pl.program_id(1) == 0)
    def _():
        _build_na_bias(rpb_ref, bias_ref, pl.program_id(0))

    head = _head_of_lane()
    band = NA_ROWS * GRID_W
    nt = (((1,), (1,)), ((), ()))
    bounds = list(range(0, band, HEAD_LANES)) + [band]

    def row_step(r, carry):
        rs = jnp.clip(r - NA_ROWS // 2, 0, rows - NA_ROWS)
        q0 = pl.multiple_of(r * GRID_W, GRID_W)
        k0 = pl.multiple_of(rs * GRID_W, GRID_W)
        qs = _stack_heads(q_ref[0, pl.ds(q0, GRID_W), :], head)
        s_loc = lax.dot_general(qs, k_ref[0, pl.ds(k0, band), :], nt, preferred_element_type=F32) + bias_ref[r - rs]
        s_ctx = lax.dot_general(qs, kc_ref[0], nt, preferred_element_type=F32)
        tiles = [s_loc[:, lo:hi] for lo, hi in zip(bounds[:-1], bounds[1:])] + [s_ctx]
        m = _lane_reduce(tiles, jnp.maximum, jnp.max)
        probs = [jnp.exp2(t - m) for t in tiles]
        inv = 1.0 / _lane_reduce(probs, jnp.add, jnp.sum)
        o = jnp.dot(jnp.concatenate(probs[:-1], axis=-1).astype(BF16), v_ref[0, pl.ds(k0, band), :],
                    preferred_element_type=F32)
        o = (o + jnp.dot(probs[-1].astype(BF16), vc_ref[0], preferred_element_type=F32)) * inv
        o_ref[0, pl.ds(q0, GRID_W), :] = _unstack_heads(o, head, GRID_W).astype(o_ref.dtype)
        return carry

    lax.fori_loop(0, rows, row_step, 0, unroll=unroll)


NA_ROW_UNROLL = 32


def _na_latent(q, k, v, kc, vc, rpb, *, unroll):
    bsz, n, na_dim = q.shape
    nc = kc.shape[1]
    groups = na_dim // HEAD_LANES
    rows = n // GRID_W
    unroll = min(unroll, rows)
    assert rows >= NA_ROWS and rows % unroll == 0
    lat = pl.BlockSpec((1, n, HEAD_LANES), lambda g, i: (i, 0, g))
    ctx = pl.BlockSpec((1, nc, HEAD_LANES), lambda g, i: (i, 0, g))
    return pl.pallas_call(
        functools.partial(_na_latent_kernel, rows=rows, unroll=unroll),
        grid=(groups, bsz),
        in_specs=[pl.BlockSpec(memory_space=pltpu.SMEM), lat, lat, lat, ctx, ctx],
        out_specs=lat,
        out_shape=jax.ShapeDtypeStruct((bsz, n, na_dim), BF16),
        scratch_shapes=[pltpu.VMEM((NA_ROWS, HEADS_PER_STEP * GRID_W, NA_ROWS * GRID_W), F32)],
        compiler_params=_params("arbitrary", "arbitrary"),
        name="na_latent",
    )(rpb.reshape(-1), q, k, v, kc, vc)


def _na_context_kernel(q_ref, k_ref, v_ref, o_ref):
    head = _head_of_lane()
    n = q_ref.shape[1]
    qs = _stack_heads(q_ref[0], head)
    s = lax.dot_general(qs, k_ref[0], (((1,), (1,)), ((), ())), preferred_element_type=F32)
    p = jnp.exp2(s - jnp.max(s, axis=-1, keepdims=True))
    denom = jnp.sum(p, axis=-1, keepdims=True)
    o = jnp.dot(p.astype(BF16), v_ref[0], preferred_element_type=F32) * (1.0 / denom)
    o_ref[0] = _unstack_heads(o, head, n).astype(o_ref.dtype)


def _na_context(q, k, v):
    bsz, n, na_dim = q.shape
    blk = pl.BlockSpec((1, n, HEAD_LANES), lambda i, g: (i, 0, g))
    return pl.pallas_call(
        _na_context_kernel,
        grid=(bsz, na_dim // HEAD_LANES),
        in_specs=[blk, blk, blk],
        out_specs=blk,
        out_shape=jax.ShapeDtypeStruct((bsz, n, na_dim), BF16),
        compiler_params=_params("parallel", "parallel"),
        name="na_context",
    )(q, k, v)


LRU_TB = 512
LRU_PAD = 8


def _rglru_kernel(xl_ref, gl_ref, xc_ref, gc_ref, cw_ref, vec_ref, wx_ref, wa_ref, yl_ref, yc_ref,
                  padl_ref, padc_ref, a_ref, b_ref, h_ref, *, n_lat, n_ctx):
    c = xl_ref.shape[-1]
    zeros = jnp.zeros((LRU_PAD, c), F32)
    for pad_ref, src_ref, n in ((padc_ref, xc_ref, n_ctx), (padl_ref, xl_ref, n_lat)):
        pad_ref[0:LRU_PAD, :] = zeros
        pad_ref[n + LRU_PAD:n + 2 * LRU_PAD, :] = zeros
        pad_ref[LRU_PAD:n + LRU_PAD, :] = src_ref[0]
    sub = lax.broadcasted_iota(jnp.int32, (SUBLANES, c), 0)

    for direction in range(2):
        reverse = direction == 1
        conv_b = vec_ref[direction, 0:1, :]
        bx = vec_ref[direction, 1:2, :]
        ba = vec_ref[direction, 2:3, :]
        neg_lam = -vec_ref[direction, 3:4, :]
        softplus = jnp.maximum(neg_lam, 0.0) + jnp.log(1.0 + jnp.exp(-jnp.abs(neg_lam)))
        decay = -LRU_C * softplus

        conv_b8 = jnp.broadcast_to(conv_b, (SUBLANES, c))
        cw8 = [jnp.broadcast_to(cw_ref[direction, k:k + 1, :], (SUBLANES, c)) for k in range(LRU_CONV)]

        def coeffs(pad_ref, base, n):
            tb = min(n, LRU_TB)
            nt = tb // SUBLANES

            def block(i, carry):
                start = pl.multiple_of(i * tb, tb)
                tiles = [pad_ref[pl.ds(start + SUBLANES * m, SUBLANES), :]
                         for m in range(nt + 2 * LRU_PAD // SUBLANES)]
                u_tiles = [conv_b8] * nt
                for k in range(LRU_CONV):
                    off = LRU_PAD + k - (0 if reverse else LRU_CONV - 1)
                    sh = _shifted_tiles(tiles, off, nt, sub)
                    u_tiles = [u_tiles[j] + cw8[k] * sh[j] for j in range(nt)]
                u = jnp.concatenate(u_tiles, axis=0)
                ub = u.astype(BF16)
                gx = 1.0 / (1.0 + jnp.exp(jnp.dot(ub, wx_ref[direction], preferred_element_type=F32) - bx))
                ga = 1.0 / (1.0 + jnp.exp(jnp.dot(ub, wa_ref[direction], preferred_element_type=F32) - ba))
                a = jnp.exp(decay * ga)
                rem = 1.0 - a * a
                coef = jnp.where(rem > 0.0, rem * lax.rsqrt(rem), 0.0)
                a_ref[pl.ds(base + start, tb), :] = a
                b_ref[pl.ds(base + start, tb), :] = coef * (gx * u)
                return carry

            lax.fori_loop(0, n // tb, block, 0)

        coeffs(padc_ref, 0, n_ctx)
        coeffs(padl_ref, n_ctx, n_lat)

        def scan(base, n, hb):
            chunks = n // SUBLANES

            def chunk(j, hb):
                jj = chunks - 1 - j if reverse else j
                rows = pl.ds(pl.multiple_of(base + jj * SUBLANES, SUBLANES), SUBLANES)
                a = a_ref[rows, :]
                b = b_ref[rows, :]
                for d in (1, 2, 4):
                    shift = SUBLANES - d if reverse else d
                    keep = (sub < SUBLANES - d) if reverse else (sub >= d)
                    ra = pltpu.roll(a, shift, 0)
                    rb = pltpu.roll(b, shift, 0)
                    b = jnp.where(keep, a * rb + b, b)
                    a = jnp.where(keep, a * ra, a)
                h = a * hb + b
                if reverse:
                    h_ref[rows, :] = h_ref[rows, :] + h
                else:
                    h_ref[rows, :] = h
                last = 0 if reverse else SUBLANES - 1
                a_last = jnp.broadcast_to(a[last:last + 1, :], (SUBLANES, c))
                b_last = jnp.broadcast_to(b[last:last + 1, :], (SUBLANES, c))
                return a_last * hb + b_last

            return lax.fori_loop(0, chunks, chunk, hb, unroll=8)

        hb = scan(0, n_ctx, jnp.zeros((SUBLANES, c), F32))
        scan(n_ctx, n_lat, hb)

    def gate_out(g_ref, y_ref, base, n):
        tb = min(n, LRU_TB)

        def block(i, carry):
            start = pl.multiple_of(i * tb, tb)
            g = g_ref[0, pl.ds(start, tb), :]
            y_ref[0, pl.ds(start, tb), :] = (_gelu_tanh(g) * h_ref[pl.ds(base + start, tb), :]).astype(y_ref.dtype)
            return carry

        lax.fori_loop(0, n // tb, block, 0)

    gate_out(gc_ref, yc_ref, 0, n_ctx)
    gate_out(gl_ref, yl_ref, n_ctx, n_lat)


def _block_diag(w):
    two, nb, m, _ = w.shape
    eye = jnp.eye(nb, dtype=w.dtype)
    return (w[:, :, :, None, :] * eye[None, :, None, :, None]).reshape(two, nb * m, nb * m)


def _rglru(xl, gl, xc, gc, conv_w, conv_b, wx, bx, wa, ba, lam):
    bsz, n_lat, c = xl.shape
    n_ctx = xc.shape[1]
    vec = jnp.stack([conv_b, bx, ba, lam], axis=1)
    wxd = _block_diag(-wx).astype(BF16)
    wad = _block_diag(-wa).astype(BF16)
    lat = pl.BlockSpec((1, n_lat, c), lambda i: (i, 0, 0))
    ctx = pl.BlockSpec((1, n_ctx, c), lambda i: (i, 0, 0))
    full = lambda a: pl.BlockSpec(a.shape, lambda i: (0,) * a.ndim)
    return pl.pallas_call(
        functools.partial(_rglru_kernel, n_lat=n_lat, n_ctx=n_ctx),
        grid=(bsz,),
        in_specs=[lat, lat, ctx, ctx, full(conv_w), full(vec), full(wxd), full(wad)],
        out_specs=[lat, ctx],
        out_shape=[jax.ShapeDtypeStruct((bsz, n_lat, c), BF16), jax.ShapeDtypeStruct((bsz, n_ctx, c), BF16)],
        scratch_shapes=[pltpu.VMEM((n_lat + 2 * LRU_PAD, c), F32), pltpu.VMEM((n_ctx + 2 * LRU_PAD, c), F32),
                        pltpu.VMEM((n_ctx + n_lat, c), F32), pltpu.VMEM((n_ctx + n_lat, c), F32),
                        pltpu.VMEM((n_ctx + n_lat, c), F32)],
        compiler_params=_params("parallel"),
        name="rglru",
    )(xl, gl, xc, gc, conv_w, vec, wxd, wad)


FF_CHUNK = 1024


def _out_mlp_body(x, ya, yb, yc, mod_ref, g2_ref, gf_ref, wo_ref, w1_ref, w2_ref, final_norm):
    ca, cb, cc = ya.shape[-1], yb.shape[-1], yc.shape[-1]
    mix = (jnp.dot(ya, wo_ref[0:ca, :], preferred_element_type=F32)
           + jnp.dot(yb, wo_ref[ca:ca + cb, :], preferred_element_type=F32)
           + jnp.dot(yc, wo_ref[ca + cb:ca + cb + cc, :], preferred_element_type=F32))
    x1 = x + mod_ref[0, 2:3, :] * mix
    h2 = _rms(x1) * g2_ref[...]
    h2 = (h2 * (1.0 + mod_ref[0, 4:5, :]) + mod_ref[0, 3:4, :]).astype(BF16)
    acc = jnp.zeros(x1.shape, F32)
    for j in range(w1_ref.shape[1] // FF_CHUNK):
        cols = slice(j * FF_CHUNK, (j + 1) * FF_CHUNK)
        hid = jnp.maximum(jnp.dot(h2, w1_ref[:, cols], preferred_element_type=F32), 0.0)
        acc = acc + jnp.dot((hid * hid).astype(BF16), w2_ref[cols, :], preferred_element_type=F32)
    out = x1 + mod_ref[0, 5:6, :] * acc
    if final_norm:
        out = _rms(out) * gf_ref[...]
    return out


def _out_mlp_kernel(x_ref, ya_ref, yb_ref, yc_ref, mod_ref, g2_ref, gf_ref, wo_ref, w1_ref, w2_ref, o_ref, *,
                    final_norm):
    o_ref[0] = _out_mlp_body(x_ref[0], ya_ref[0], yb_ref[0], yc_ref[0], mod_ref, g2_ref, gf_ref,
                             wo_ref, w1_ref, w2_ref, final_norm)


def _conformer_tile(u_ref, w8_ref, bias, lg_ref, lb_ref, ya_ref, slot, tile, *, n, tm, sub):
    half = CONV_WIDTH // 2
    out_tiles = CONV_TB // SUBLANES
    win_tiles = (CONV_TB + 2 * CONV_PAD) // SUBLANES
    pad_tiles = CONV_PAD // SUBLANES
    blocks = tm // CONV_TB
    first = tile == 0
    last = tile == n // tm - 1
    zero = jnp.zeros((SUBLANES, u_ref.shape[-1]), F32)
    done = []
    for blk in range(blocks):
        base = tile * tm + (CONV_TB * blk - CONV_PAD)
        tiles = []
        for m in range(win_tiles):
            start = base + SUBLANES * m
            if blk == 0 and m < pad_tiles:
                t = u_ref[0, pl.ds(pl.multiple_of(jnp.maximum(start, 0), SUBLANES), SUBLANES), :]
                t = jnp.where(first, zero, t)
            elif blk == blocks - 1 and m >= win_tiles - pad_tiles:
                t = u_ref[0, pl.ds(pl.multiple_of(jnp.minimum(start, n - SUBLANES), SUBLANES), SUBLANES), :]
                t = jnp.where(last, zero, t)
            else:
                t = u_ref[0, pl.ds(pl.multiple_of(start, SUBLANES), SUBLANES), :]
            tiles.append(t)
        acc = [bias] * out_tiles
        for r in range(SUBLANES):
            sh = _shifted_tiles(tiles, r, win_tiles - 1, sub)
            for a in range(win_tiles - out_tiles):
                k = r + SUBLANES * a - (CONV_PAD - half)
                if 0 <= k < CONV_WIDTH:
                    w = w8_ref[k]
                    acc = [acc[j] + w * sh[j + a] for j in range(out_tiles)]
        done.extend(acc)
        if len(done) == CONV_NORM_TB // SUBLANES:
            conv = jnp.concatenate(done, axis=0)
            done = []
            mu = jnp.mean(conv, axis=-1, keepdims=True)
            cen = conv - mu
            var = jnp.mean(cen * cen, axis=-1, keepdims=True)
            y = cen * lax.rsqrt(var + EPS) * lg_ref[...] + lb_ref[...]
            row0 = CONV_TB * (blk + 1) - CONV_NORM_TB
            ya_ref[slot, row0:row0 + CONV_NORM_TB, :] = (y * _sigmoid(y)).astype(ya_ref.dtype)


def _conv_out_mlp_kernel(x_ref, u_ref, yb_ref, yc_ref, cw_ref, cb_ref, lg_ref, lb_ref, mod_ref, g2_ref, gf_ref,
                         wo_ref, w1_ref, w2_ref, o_ref, ya_ref, w8_ref, *, final_norm, n, tm):
    c = u_ref.shape[-1]
    tiles_per_seq = n // tm
    step = pl.program_id(0) * tiles_per_seq + pl.program_id(1)
    cur = step % 2
    for k in range(CONV_WIDTH):
        w8_ref[k] = jnp.broadcast_to(cw_ref[k:k + 1, :], (SUBLANES, c))
    sub = lax.broadcasted_iota(jnp.int32, (SUBLANES, c), 0)
    bias = jnp.broadcast_to(cb_ref[...], (SUBLANES, c))
    conv = functools.partial(_conformer_tile, u_ref, w8_ref, bias, lg_ref, lb_ref, ya_ref, n=n, tm=tm, sub=sub)

    @pl.when(step == 0)
    def _():
        conv(0, 0)

    o_ref[0] = _out_mlp_body(x_ref[0], ya_ref[cur], yb_ref[0], yc_ref[0], mod_ref, g2_ref, gf_ref,
                             wo_ref, w1_ref, w2_ref, final_norm)
    conv(1 - cur, (pl.program_id(1) + 1) % tiles_per_seq)


def _out_mlp(x, ya, yb, yc, mod, g2, gf, wo, w1, w2, layer, *, final_norm):
    b, n, d = x.shape
    tm = min(n, 512)
    per_batch = mod.shape[0] != 1
    tok = lambda a: pl.BlockSpec((1, tm, a.shape[-1]), lambda i, j: (i, j, 0))
    resident = lambda a: _layer_weight(a, layer, pipeline_mode=pl.Buffered(1))
    return pl.pallas_call(
        functools.partial(_out_mlp_kernel, final_norm=final_norm),
        grid=(b, n // tm),
        in_specs=[tok(x), tok(ya), tok(yb), tok(yc),
                  pl.BlockSpec((1, 6, d), (lambda i, j: (i, 0, 0)) if per_batch else (lambda i, j: (0, 0, 0))),
                  pl.BlockSpec((1, d), lambda i, j: (0, 0)), pl.BlockSpec((1, d), lambda i, j: (0, 0)),
                  resident(wo), resident(w1), resident(w2)],
        out_specs=tok(x),
        out_shape=jax.ShapeDtypeStruct((b, n, d), F32),
        compiler_params=_params("parallel", "parallel"),
        name="out_proj_mlp",
    )(x, ya, yb, yc, mod, g2, gf, wo, w1, w2)


def _conv_out_mlp(x, u, yb, yc, conv_w, conv_b, ln_g, ln_b, mod, g2, gf, wo, w1, w2, layer, *, final_norm):
    b, n, d = x.shape
    c = u.shape[-1]
    tm = min(n, 512)
    nt = n // tm
    assert tm % CONV_NORM_TB == 0 and CONV_NORM_TB % CONV_TB == 0
    tok = lambda a: pl.BlockSpec((1, tm, a.shape[-1]), lambda i, j: (i, j, 0))
    row = lambda width: pl.BlockSpec((1, width), lambda i, j: (0, 0))
    resident = lambda a: _layer_weight(a, layer, pipeline_mode=pl.Buffered(1))
    next_batch = lambda i, j: (jnp.minimum(i + (j + 1) // nt, b - 1), 0, 0)
    return pl.pallas_call(
        functools.partial(_conv_out_mlp_kernel, final_norm=final_norm, n=n, tm=tm),
        grid=(b, nt),
        in_specs=[tok(x), pl.BlockSpec((1, n, c), next_batch), tok(yb), tok(yc),
                  pl.BlockSpec(conv_w.shape, lambda i, j: (0, 0)), row(c), row(c), row(c),
                  pl.BlockSpec((1, 6, d), lambda i, j: (i, 0, 0)), row(d), row(d),
                  resident(wo), resident(w1), resident(w2)],
        out_specs=tok(x),
        out_shape=jax.ShapeDtypeStruct((b, n, d), F32),
        scratch_shapes=[pltpu.VMEM((2, tm, c), BF16), pltpu.VMEM((CONV_WIDTH, SUBLANES, c), F32)],
        compiler_params=_params("arbitrary", "arbitrary"),
        name="conv_out_proj_mlp",
    )(x, u, yb, yc, conv_w, conv_b.reshape(1, c), ln_g.reshape(1, c), ln_b.reshape(1, c), mod, g2, gf, wo, w1, w2)


def kernel(x, c, ctx, c_ctx, norm1_g, norm2_g, ada_w, ada_b, w_in, w_out, conv_w, conv_b, conv_ln_g, conv_ln_b,
           na_rpb, lru_conv_w, lru_conv_b, lru_wx, lru_bx, lru_wa, lru_ba, lru_lambda, mlp_w1, mlp_w2, final_g):
    depth = w_in.shape[0]
    bsz, _, d = x.shape
    conv_dim = conv_w.shape[-1]
    lru_dim = lru_conv_w.shape[-1]
    na_dim = NA_HEADS * NA_HEAD_DIM
    dims = dict(conv_dim=conv_dim, na_dim=na_dim, lru_dim=lru_dim)

    cond = jnp.concatenate([c, c_ctx[None, :]], axis=0)
    cond = jnp.pad(cond, ((0, -(bsz + 1) % SUBLANES), (0, 0)))
    mod_all = _modulation(cond, ada_w, ada_b)
    w_in_b, w_out_b = w_in.astype(BF16), w_out.astype(BF16)
    w1_b, w2_b = mlp_w1.astype(BF16), mlp_w2.astype(BF16)
    cx = ctx
    for l in range(depth):
        last = l == depth - 1
        mod = mod_all[l, :bsz].reshape(bsz, 6, d)
        mod_c = mod_all[l, bsz:bsz + 1].reshape(1, 6, d)
        g1 = norm1_g[l].reshape(1, d)

        lat = _in_proj(x, mod, g1, w_in_b, l, **dims)
        cxp = _in_proj(cx, mod_c, g1, w_in_b, l, **dims, want=("k", "v", "rx", "rg") if last else IN_PROJ_OUTPUTS)

        y_b = _na_latent(lat["q"], lat["k"], lat["v"], cxp["k"], cxp["v"], na_rpb[l], unroll=NA_ROW_UNROLL)
        y_c, yc_c = _rglru(lat["rx"], lat["rg"], cxp["rx"], cxp["rg"], lru_conv_w[l], lru_conv_b[l],
                           lru_wx[l], lru_bx[l], lru_wa[l], lru_ba[l], lru_lambda[l])

        g2 = norm2_g[l].reshape(1, d)
        gf = final_g.reshape(1, d)
        x = _conv_out_mlp(x, lat["u"], y_b, y_c, conv_w[l], conv_b[l], conv_ln_g[l], conv_ln_b[l], mod, g2, gf,
                          w_out_b, w1_b, w2_b, l, final_norm=last)
        if not last:
            yc_a = _conformer(cxp["u"], conv_w[l], conv_b[l], conv_ln_g[l], conv_ln_b[l])
            yc_b = _na_context(cxp["q"], cxp["k"], cxp["v"])
            cx = _out_mlp(cx, yc_a, yc_b, yc_c, mod_c, g2, gf, w_out_b, w1_b, w2_b, l, final_norm=False)
    return x
```

```python
import functools

import jax
import jax.numpy as jnp
from jax import lax
from jax.experimental import pallas as pl
from jax.experimental.pallas import tpu as pltpu

GRID_W = 64
CONV_WIDTH = 31
NA_HEADS = 8
NA_HEAD_DIM = 64
NA_ROWS = 8
NA_COLS = 16
LRU_BLOCKS = 4
LRU_CONV = 4
LRU_C = 8.0
EPS = 1e-6
NEG_INF = -1e30
LOG2_E = 1.4426950408889634

SUBLANES = 8
LANES = 128
HEADS_PER_STEP = 4
HEAD_LANES = HEADS_PER_STEP * NA_HEAD_DIM
VMEM_LIMIT_BYTES = 56 * 1024 * 1024

F32 = jnp.float32
BF16 = jnp.bfloat16


def _sigmoid(x):
    return 1.0 / (1.0 + jnp.exp(-x))


def _gelu_tanh(x):
    return 0.5 * x * (1.0 + jnp.tanh(0.7978845608028654 * (x + 0.044715 * (x * x * x))))


def _rms(x):
    return x * lax.rsqrt(jnp.mean(x * x, axis=-1, keepdims=True) + EPS)


def _layer_weight(w_all, layer, **kwargs):
    return pl.BlockSpec((None,) + w_all.shape[1:], lambda i, j: (layer, 0, 0), **kwargs)


def _params(*sem):
    return pltpu.CompilerParams(dimension_semantics=sem, vmem_limit_bytes=VMEM_LIMIT_BYTES)


def _mod_kernel(c_ref, w_ref, b_ref, o_ref):
    c = c_ref[...]
    s = c * _sigmoid(c)
    o_ref[0] = jnp.dot(s, w_ref[0], preferred_element_type=F32,
                       precision=lax.Precision.HIGHEST) + b_ref[0]


def _modulation(cc, ada_w, ada_b):
    depth, d, n = ada_w.shape
    r = cc.shape[0]
    tn = n // 4
    return pl.pallas_call(
        _mod_kernel,
        grid=(depth, n // tn),
        in_specs=[pl.BlockSpec((r, d), lambda l, j: (0, 0)),
                  pl.BlockSpec((1, d, tn), lambda l, j: (l, 0, j)),
                  pl.BlockSpec((1, 1, tn), lambda l, j: (l, 0, j))],
        out_specs=pl.BlockSpec((1, r, tn), lambda l, j: (l, 0, j)),
        out_shape=jax.ShapeDtypeStruct((depth, r, n), F32),
        compiler_params=_params("parallel", "parallel"),
        name="adaln_modulation",
    )(cc, ada_w, ada_b.reshape(depth, 1, n))


IN_PROJ_OUTPUTS = ("u", "q", "k", "v", "rx", "rg")


def _in_proj_kernel(x_ref, mod_ref, g_ref, w_ref, *out_refs, conv_dim, na_dim, lru_dim, want):
    out = dict(zip(want, out_refs))
    x = x_ref[0]
    h = _rms(x) * g_ref[...]
    h = h * (1.0 + mod_ref[0, 1:2, :]) + mod_ref[0, 0:1, :]
    hb = h.astype(BF16)

    def proj(off, width):
        return jnp.dot(hb, w_ref[:, off:off + width], preferred_element_type=F32)

    q_off = 2 * conv_dim
    if "u" in out:
        out["u"][0] = proj(0, conv_dim) * _sigmoid(proj(conv_dim, conv_dim))
    if "q" in out:
        out["q"][0] = (proj(q_off, na_dim) * (NA_HEAD_DIM ** -0.5 * LOG2_E)).astype(BF16)
    if "k" in out:
        out["k"][0] = proj(q_off + na_dim, na_dim).astype(BF16)
    if "v" in out:
        out["v"][0] = proj(q_off + 2 * na_dim, na_dim).astype(BF16)
    if "rx" in out:
        out["rx"][0] = proj(q_off + 3 * na_dim, lru_dim)
    if "rg" in out:
        out["rg"][0] = proj(q_off + 3 * na_dim + lru_dim, lru_dim)


IN_PROJ_TM = 1024


def _in_proj(x, mod, g, w_all, layer, *, conv_dim, na_dim, lru_dim, want=IN_PROJ_OUTPUTS):
    b, n, d = x.shape
    tm = min(n, IN_PROJ_TM)
    per_batch = mod.shape[0] != 1
    kinds = dict(u=(conv_dim, F32), q=(na_dim, BF16), k=(na_dim, BF16), v=(na_dim, BF16),
                 rx=(lru_dim, F32), rg=(lru_dim, F32))
    outs = pl.pallas_call(
        functools.partial(_in_proj_kernel, conv_dim=conv_dim, na_dim=na_dim, lru_dim=lru_dim, want=want),
        grid=(b, n // tm),
        in_specs=[pl.BlockSpec((1, tm, d), lambda i, j: (i, j, 0)),
                  pl.BlockSpec((1, 6, d), (lambda i, j: (i, 0, 0)) if per_batch else (lambda i, j: (0, 0, 0))),
                  pl.BlockSpec((1, d), lambda i, j: (0, 0)),
                  _layer_weight(w_all, layer)],
        out_specs=[pl.BlockSpec((1, tm, kinds[name][0]), lambda i, j: (i, j, 0)) for name in want],
        out_shape=[jax.ShapeDtypeStruct((b, n, kinds[name][0]), kinds[name][1]) for name in want],
        compiler_params=_params("parallel", "parallel"),
        name="in_proj",
    )(x, mod, g, w_all)
    return dict(zip(want, outs))


CONV_TB = 32
CONV_PAD = 16
CONV_NORM_TB = 128


def _shifted_tiles(tiles, off, count, sub):
    a, r = divmod(off, SUBLANES)
    if r == 0:
        return tiles[a:a + count]
    rolled = [pltpu.roll(t, SUBLANES - r, 0) for t in tiles[a:a + count + 1]]
    keep = sub < SUBLANES - r
    return [jnp.where(keep, rolled[i], rolled[i + 1]) for i in range(count)]


def _conformer_kernel(u_ref, w_ref, b_ref, lg_ref, lb_ref, o_ref, pad_ref, w8_ref, conv_ref, *, n):
    c = u_ref.shape[-1]
    zeros = jnp.zeros((CONV_PAD, c), F32)
    pad_ref[0:CONV_PAD, :] = zeros
    pad_ref[n + CONV_PAD:n + 2 * CONV_PAD, :] = zeros
    pad_ref[CONV_PAD:n + CONV_PAD, :] = u_ref[0]
    for k in range(CONV_WIDTH):
        w8_ref[k] = jnp.broadcast_to(w_ref[k:k + 1, :], (SUBLANES, c))
    sub = lax.broadcasted_iota(jnp.int32, (SUBLANES, c), 0)
    bias = jnp.broadcast_to(b_ref[...], (SUBLANES, c))
    half = CONV_WIDTH // 2
    out_tiles = CONV_TB // SUBLANES
    win_tiles = (CONV_TB + 2 * CONV_PAD) // SUBLANES

    def block(i, carry):
        start = pl.multiple_of(i * CONV_TB, CONV_TB)
        tiles = [pad_ref[pl.ds(start + SUBLANES * m, SUBLANES), :] for m in range(win_tiles)]
        acc = [bias] * out_tiles
        for r in range(SUBLANES):
            sh = _shifted_tiles(tiles, r, win_tiles - 1, sub)
            for a in range(win_tiles - out_tiles):
                k = r + SUBLANES * a - (CONV_PAD - half)
                if 0 <= k < CONV_WIDTH:
                    w = w8_ref[k]
                    acc = [acc[j] + w * sh[j + a] for j in range(out_tiles)]
        conv_ref[pl.ds(start, CONV_TB), :] = jnp.concatenate(acc, axis=0)
        return carry

    lax.fori_loop(0, n // CONV_TB, block, 0)

    def norm_block(i, carry):
        start = pl.multiple_of(i * CONV_NORM_TB, CONV_NORM_TB)
        conv = conv_ref[pl.ds(start, CONV_NORM_TB), :]
        mu = jnp.mean(conv, axis=-1, keepdims=True)
        cen = conv - mu
        var = jnp.mean(cen * cen, axis=-1, keepdims=True)
        y = cen * lax.rsqrt(var + EPS) * lg_ref[...] + lb_ref[...]
        o_ref[0, pl.ds(start, CONV_NORM_TB), :] = (y * _sigmoid(y)).astype(o_ref.dtype)
        return carry

    lax.fori_loop(0, n // CONV_NORM_TB, norm_block, 0, unroll=4)


def _conformer(u, w, b, ln_g, ln_b):
    bsz, n, c = u.shape
    row = pl.BlockSpec((1, c), lambda i: (0, 0))
    return pl.pallas_call(
        functools.partial(_conformer_kernel, n=n),
        grid=(bsz,),
        in_specs=[pl.BlockSpec((1, n, c), lambda i: (i, 0, 0)),
                  pl.BlockSpec(w.shape, lambda i: (0, 0)), row, row, row],
        out_specs=pl.BlockSpec((1, n, c), lambda i: (i, 0, 0)),
        out_shape=jax.ShapeDtypeStruct((bsz, n, c), BF16),
        scratch_shapes=[pltpu.VMEM((n + 2 * CONV_PAD, c), F32), pltpu.VMEM((CONV_WIDTH, SUBLANES, c), F32),
                        pltpu.VMEM((n, c), F32)],
        compiler_params=_params("parallel"),
        name="conformer_conv",
    )(u, w, b.reshape(1, c), ln_g.reshape(1, c), ln_b.reshape(1, c))


def _head_of_lane():
    return lax.broadcasted_iota(jnp.int32, (1, HEAD_LANES), 1) // NA_HEAD_DIM


def _stack_heads(qr, head):
    zero = jnp.zeros_like(qr)
    return jnp.concatenate([jnp.where(head == h, qr, zero) for h in range(HEADS_PER_STEP)], axis=0)


def _unstack_heads(o, head, m):
    out = o[0:m]
    for h in range(1, HEADS_PER_STEP):
        out = jnp.where(head == h, o[h * m:(h + 1) * m], out)
    return out


def _build_na_bias(rpb_ref, bias_ref, group):
    qi = lax.broadcasted_iota(jnp.int32, (GRID_W, GRID_W), 0)
    ki = lax.broadcasted_iota(jnp.int32, (GRID_W, GRID_W), 1)
    rel = ki - qi
    col_start = jnp.clip(qi - NA_COLS // 2, 0, GRID_W - NA_COLS)
    valid = (ki >= col_start) & (ki < col_start + NA_COLS)
    n_r = 2 * NA_ROWS - 1
    n_c = 2 * NA_COLS - 1

    def per_head(h, carry):
        q0 = pl.multiple_of(h * GRID_W, GRID_W)
        for ri in range(n_r):
            base = ((group * HEADS_PER_STEP + h) * n_r + ri) * n_c
            t = jnp.full((GRID_W, GRID_W), NEG_INF, F32)
            for dc in range(n_c):
                t = jnp.where(rel == dc - (NA_COLS - 1), rpb_ref[base + dc] * LOG2_E, t)
            t = jnp.where(valid, t, NEG_INF)
            for d in range(NA_ROWS):
                j = ri + d - (NA_ROWS - 1)
                if 0 <= j < NA_ROWS:
                    bias_ref[d, pl.ds(q0, GRID_W), j * GRID_W:(j + 1) * GRID_W] = t
        return carry

    lax.fori_loop(0, HEADS_PER_STEP, per_head, 0)


def _lane_reduce(tiles, combine, reduce):
    by_width = {}
    for t in tiles:
        w = t.shape[-1]
        by_width[w] = t if w not in by_width else combine(by_width[w], t)
    parts = [reduce(t, axis=-1, keepdims=True) for t in by_width.values()]
    out = parts[0]
    for part in parts[1:]:
        out = combine(out, part)
    return out


def _na_latent_kernel(rpb_ref, q_ref, k_ref, v_ref, kc_ref, vc_ref, o_ref, bias_ref, *, rows, unroll):
    @pl.when(pl.program_id(1) == 0)
    def _():
        _build_na_bias(rpb_ref, bias_ref, pl.program_id(0))

    head = _head_of_lane()
    band = NA_ROWS * GRID_W
    nt = (((1,), (1,)), ((), ()))
    bounds = list(range(0, band, HEAD_LANES)) + [band]

    def row_step(r, carry):
        rs = jnp.clip(r - NA_ROWS // 2, 0, rows - NA_ROWS)
        q0 = pl.multiple_of(r * GRID_W, GRID_W)
        k0 = pl.multiple_of(rs * GRID_W, GRID_W)
        qs = _stack_heads(q_ref[0, pl.ds(q0, GRID_W), :], head)
        s_loc = lax.dot_general(qs, k_ref[0, pl.ds(k0, band), :], nt, preferred_element_type=F32) + bias_ref[r - rs]
        s_ctx = lax.dot_general(qs, kc_ref[0], nt, preferred_element_type=F32)
        tiles = [s_loc[:, lo:hi] for lo, hi in zip(bounds[:-1], bounds[1:])] + [s_ctx]
        m = _lane_reduce(tiles, jnp.maximum, jnp.max)
        probs = [jnp.exp2(t - m) for t in tiles]
        inv = 1.0 / _lane_reduce(probs, jnp.add, jnp.sum)
        o = jnp.dot(jnp.concatenate(probs[:-1], axis=-1).astype(BF16), v_ref[0, pl.ds(k0, band), :],
                    preferred_element_type=F32)
        o = (o + jnp.dot(probs[-1].astype(BF16), vc_ref[0], preferred_element_type=F32)) * inv
        o_ref[0, pl.ds(q0, GRID_W), :] = _unstack_heads(o, head, GRID_W).astype(o_ref.dtype)
        return carry

    lax.fori_loop(0, rows, row_step, 0, unroll=unroll)


NA_ROW_UNROLL = 32


def _na_latent(q, k, v, kc, vc, rpb, *, unroll):
    bsz, n, na_dim = q.shape
    nc = kc.shape[1]
    groups = na_dim // HEAD_LANES
    rows = n // GRID_W
    unroll = min(unroll, rows)
    assert rows >= NA_ROWS and rows % unroll == 0
    lat = pl.BlockSpec((1, n, HEAD_LANES), lambda g, i: (i, 0, g))
    ctx = pl.BlockSpec((1, nc, HEAD_LANES), lambda g, i: (i, 0, g))
    return pl.pallas_call(
        functools.partial(_na_latent_kernel, rows=rows, unroll=unroll),
        grid=(groups, bsz),
        in_specs=[pl.BlockSpec(memory_space=pltpu.SMEM), lat, lat, lat, ctx, ctx],
        out_specs=lat,
        out_shape=jax.ShapeDtypeStruct((bsz, n, na_dim), BF16),
        scratch_shapes=[pltpu.VMEM((NA_ROWS, HEADS_PER_STEP * GRID_W, NA_ROWS * GRID_W), F32)],
        compiler_params=_params("arbitrary", "arbitrary"),
        name="na_latent",
    )(rpb.reshape(-1), q, k, v, kc, vc)


def _na_context_kernel(q_ref, k_ref, v_ref, o_ref):
    head = _head_of_lane()
    n = q_ref.shape[1]
    qs = _stack_heads(q_ref[0], head)
    s = lax.dot_general(qs, k_ref[0], (((1,), (1,)), ((), ())), preferred_element_type=F32)
    p = jnp.exp2(s - jnp.max(s, axis=-1, keepdims=True))
    denom = jnp.sum(p, axis=-1, keepdims=True)
    o = jnp.dot(p.astype(BF16), v_ref[0], preferred_element_type=F32) * (1.0 / denom)
    o_ref[0] = _unstack_heads(o, head, n).astype(o_ref.dtype)


def _na_context(q, k, v):
    bsz, n, na_dim = q.shape
    blk = pl.BlockSpec((1, n, HEAD_LANES), lambda i, g: (i, 0, g))
    return pl.pallas_call(
        _na_context_kernel,
        grid=(bsz, na_dim // HEAD_LANES),
        in_specs=[blk, blk, blk],
        out_specs=blk,
        out_shape=jax.ShapeDtypeStruct((bsz, n, na_dim), BF16),
        compiler_params=_params("parallel", "parallel"),
        name="na_context",
    )(q, k, v)


LRU_TB = 1024
LRU_SCAN_UNROLL = 16
LRU_PAD = 8


def _rglru_kernel(xl_ref, gl_ref, xc_ref, gc_ref, cw_ref, vec_ref, wx_ref, wa_ref, yl_ref, yc_ref,
                  padl_ref, padc_ref, a_ref, b_ref, h_ref, *, n_lat, n_ctx):
    c = xl_ref.shape[-1]
    zeros = jnp.zeros((LRU_PAD, c), F32)
    for pad_ref, src_ref, n in ((padc_ref, xc_ref, n_ctx), (padl_ref, xl_ref, n_lat)):
        pad_ref[0:LRU_PAD, :] = zeros
        pad_ref[n + LRU_PAD:n + 2 * LRU_PAD, :] = zeros
        pad_ref[LRU_PAD:n + LRU_PAD, :] = src_ref[0]
    sub = lax.broadcasted_iota(jnp.int32, (SUBLANES, c), 0)

    for direction in range(2):
        reverse = direction == 1
        conv_b = vec_ref[direction, 0:1, :]
        bx = vec_ref[direction, 1:2, :]
        ba = vec_ref[direction, 2:3, :]
        neg_lam = -vec_ref[direction, 3:4, :]
        softplus = jnp.maximum(neg_lam, 0.0) + jnp.log(1.0 + jnp.exp(-jnp.abs(neg_lam)))
        decay = -LRU_C * softplus

        conv_b8 = jnp.broadcast_to(conv_b, (SUBLANES, c))
        cw8 = [jnp.broadcast_to(cw_ref[direction, k:k + 1, :], (SUBLANES, c)) for k in range(LRU_CONV)]

        def coeffs(pad_ref, base, n):
            tb = min(n, LRU_TB)
            nt = tb // SUBLANES

            def block(i, carry):
                start = pl.multiple_of(i * tb, tb)
                tiles = [pad_ref[pl.ds(start + SUBLANES * m, SUBLANES), :]
                         for m in range(nt + 2 * LRU_PAD // SUBLANES)]
                u_tiles = [conv_b8] * nt
                for k in range(LRU_CONV):
                    off = LRU_PAD + k - (0 if reverse else LRU_CONV - 1)
                    sh = _shifted_tiles(tiles, off, nt, sub)
                    u_tiles = [u_tiles[j] + cw8[k] * sh[j] for j in range(nt)]
                u = jnp.concatenate(u_tiles, axis=0)
                ub = u.astype(BF16)
                gx = 1.0 / (1.0 + jnp.exp(jnp.dot(ub, wx_ref[direction], preferred_element_type=F32) - bx))
                ga = 1.0 / (1.0 + jnp.exp(jnp.dot(ub, wa_ref[direction], preferred_element_type=F32) - ba))
                a = jnp.exp(decay * ga)
                rem = 1.0 - a * a
                coef = jnp.where(rem > 0.0, rem * lax.rsqrt(rem), 0.0)
                a_ref[pl.ds(base + start, tb), :] = a
                b_ref[pl.ds(base + start, tb), :] = coef * (gx * u)
                return carry

            lax.fori_loop(0, n // tb, block, 0)

        coeffs(padc_ref, 0, n_ctx)
        coeffs(padl_ref, n_ctx, n_lat)

        def scan(base, n, hb):
            chunks = n // SUBLANES

            def chunk(j, hb):
                jj = chunks - 1 - j if reverse else j
                rows = pl.ds(pl.multiple_of(base + jj * SUBLANES, SUBLANES), SUBLANES)
                a = a_ref[rows, :]
                b = b_ref[rows, :]
                for d in (1, 2, 4):
                    shift = SUBLANES - d if reverse else d
                    keep = (sub < SUBLANES - d) if reverse else (sub >= d)
                    ra = pltpu.roll(a, shift, 0)
                    rb = pltpu.roll(b, shift, 0)
                    b = jnp.where(keep, a * rb + b, b)
                    a = jnp.where(keep, a * ra, a)
                h = a * hb + b
                if reverse:
                    h_ref[rows, :] = h_ref[rows, :] + h
                else:
                    h_ref[rows, :] = h
                last = 0 if reverse else SUBLANES - 1
                a_last = jnp.broadcast_to(a[last:last + 1, :], (SUBLANES, c))
                b_last = jnp.broadcast_to(b[last:last + 1, :], (SUBLANES, c))
                return a_last * hb + b_last

            return lax.fori_loop(0, chunks, chunk, hb, unroll=LRU_SCAN_UNROLL)

        hb = scan(0, n_ctx, jnp.zeros((SUBLANES, c), F32))
        scan(n_ctx, n_lat, hb)

    def gate_out(g_ref, y_ref, base, n):
        tb = min(n, LRU_TB)

        def block(i, carry):
            start = pl.multiple_of(i * tb, tb)
            g = g_ref[0, pl.ds(start, tb), :]
            y_ref[0, pl.ds(start, tb), :] = (_gelu_tanh(g) * h_ref[pl.ds(base + start, tb), :]).astype(y_ref.dtype)
            return carry

        lax.fori_loop(0, n // tb, block, 0)

    gate_out(gc_ref, yc_ref, 0, n_ctx)
    gate_out(gl_ref, yl_ref, n_ctx, n_lat)


def _block_diag(w):
    two, nb, m, _ = w.shape
    eye = jnp.eye(nb, dtype=w.dtype)
    return (w[:, :, :, None, :] * eye[None, :, None, :, None]).reshape(two, nb * m, nb * m)


def _rglru(xl, gl, xc, gc, conv_w, conv_b, wx, bx, wa, ba, lam):
    bsz, n_lat, c = xl.shape
    n_ctx = xc.shape[1]
    vec = jnp.stack([conv_b, bx, ba, lam], axis=1)
    wxd = _block_diag(-wx).astype(BF16)
    wad = _block_diag(-wa).astype(BF16)
    lat = pl.BlockSpec((1, n_lat, c), lambda i: (i, 0, 0))
    ctx = pl.BlockSpec((1, n_ctx, c), lambda i: (i, 0, 0))
    full = lambda a: pl.BlockSpec(a.shape, lambda i: (0,) * a.ndim)
    return pl.pallas_call(
        functools.partial(_rglru_kernel, n_lat=n_lat, n_ctx=n_ctx),
        grid=(bsz,),
        in_specs=[lat, lat, ctx, ctx, full(conv_w), full(vec), full(wxd), full(wad)],
        out_specs=[lat, ctx],
        out_shape=[jax.ShapeDtypeStruct((bsz, n_lat, c), BF16), jax.ShapeDtypeStruct((bsz, n_ctx, c), BF16)],
        scratch_shapes=[pltpu.VMEM((n_lat + 2 * LRU_PAD, c), F32), pltpu.VMEM((n_ctx + 2 * LRU_PAD, c), F32),
                        pltpu.VMEM((n_ctx + n_lat, c), F32), pltpu.VMEM((n_ctx + n_lat, c), F32),
                        pltpu.VMEM((n_ctx + n_lat, c), F32)],
        compiler_params=_params("parallel"),
        name="rglru",
    )(xl, gl, xc, gc, conv_w, vec, wxd, wad)


FF_CHUNK = 1024


def _out_mlp_body(x, ya, yb, yc, mod_ref, g2_ref, gf_ref, wo_ref, w1_ref, w2_ref, final_norm):
    ca, cb, cc = ya.shape[-1], yb.shape[-1], yc.shape[-1]
    mix = (jnp.dot(ya, wo_ref[0:ca, :], preferred_element_type=F32)
           + jnp.dot(yb, wo_ref[ca:ca + cb, :], preferred_element_type=F32)
           + jnp.dot(yc, wo_ref[ca + cb:ca + cb + cc, :], preferred_element_type=F32))
    x1 = x + mod_ref[0, 2:3, :] * mix
    h2 = _rms(x1) * g2_ref[...]
    h2 = (h2 * (1.0 + mod_ref[0, 4:5, :]) + mod_ref[0, 3:4, :]).astype(BF16)
    acc = jnp.zeros(x1.shape, F32)
    for j in range(w1_ref.shape[1] // FF_CHUNK):
        cols = slice(j * FF_CHUNK, (j + 1) * FF_CHUNK)
        hid = jnp.maximum(jnp.dot(h2, w1_ref[:, cols], preferred_element_type=F32), 0.0)
        acc = acc + jnp.dot((hid * hid).astype(BF16), w2_ref[cols, :], preferred_element_type=F32)
    out = x1 + mod_ref[0, 5:6, :] * acc
    if final_norm:
        out = _rms(out) * gf_ref[...]
    return out


def _out_mlp_kernel(x_ref, ya_ref, yb_ref, yc_ref, mod_ref, g2_ref, gf_ref, wo_ref, w1_ref, w2_ref, o_ref, *,
                    final_norm):
    o_ref[0] = _out_mlp_body(x_ref[0], ya_ref[0], yb_ref[0], yc_ref[0], mod_ref, g2_ref, gf_ref,
                             wo_ref, w1_ref, w2_ref, final_norm)


def _conformer_tile(u_ref, w8_ref, bias, lg_ref, lb_ref, ya_ref, slot, tile, *, n, tm, sub):
    half = CONV_WIDTH // 2
    out_tiles = CONV_TB // SUBLANES
    win_tiles = (CONV_TB + 2 * CONV_PAD) // SUBLANES
    pad_tiles = CONV_PAD // SUBLANES
    blocks = tm // CONV_TB
    first = tile == 0
    last = tile == n // tm - 1
    zero = jnp.zeros((SUBLANES, u_ref.shape[-1]), F32)
    done = []
    for blk in range(blocks):
        base = tile * tm + (CONV_TB * blk - CONV_PAD)
        tiles = []
        for m in range(win_tiles):
            start = base + SUBLANES * m
            if blk == 0 and m < pad_tiles:
                t = u_ref[0, pl.ds(pl.multiple_of(jnp.maximum(start, 0), SUBLANES), SUBLANES), :]
                t = jnp.where(first, zero, t)
            elif blk == blocks - 1 and m >= win_tiles - pad_tiles:
                t = u_ref[0, pl.ds(pl.multiple_of(jnp.minimum(start, n - SUBLANES), SUBLANES), SUBLANES), :]
                t = jnp.where(last, zero, t)
            else:
                t = u_ref[0, pl.ds(pl.multiple_of(start, SUBLANES), SUBLANES), :]
            tiles.append(t)
        acc = [bias] * out_tiles
        for r in range(SUBLANES):
            sh = _shifted_tiles(tiles, r, win_tiles - 1, sub)
            for a in range(win_tiles - out_tiles):
                k = r + SUBLANES * a - (CONV_PAD - half)
                if 0 <= k < CONV_WIDTH:
                    w = w8_ref[k]
                    acc = [acc[j] + w * sh[j + a] for j in range(out_tiles)]
        done.extend(acc)
        if len(done) == CONV_NORM_TB // SUBLANES:
            conv = jnp.concatenate(done, axis=0)
            done = []
            mu = jnp.mean(conv, axis=-1, keepdims=True)
            cen = conv - mu
            var = jnp.mean(cen * cen, axis=-1, keepdims=True)
            y = cen * lax.rsqrt(var + EPS) * lg_ref[...] + lb_ref[...]
            row0 = CONV_TB * (blk + 1) - CONV_NORM_TB
            ya_ref[slot, row0:row0 + CONV_NORM_TB, :] = (y * _sigmoid(y)).astype(ya_ref.dtype)


def _conv_out_mlp_kernel(x_ref, u_ref, yb_ref, yc_ref, cw_ref, cb_ref, lg_ref, lb_ref, mod_ref, g2_ref, gf_ref,
                         wo_ref, w1_ref, w2_ref, o_ref, ya_ref, w8_ref, *, final_norm, n, tm):
    c = u_ref.shape[-1]
    tiles_per_seq = n // tm
    step = pl.program_id(0) * tiles_per_seq + pl.program_id(1)
    cur = step % 2
    for k in range(CONV_WIDTH):
        w8_ref[k] = jnp.broadcast_to(cw_ref[k:k + 1, :], (SUBLANES, c))
    sub = lax.broadcasted_iota(jnp.int32, (SUBLANES, c), 0)
    bias = jnp.broadcast_to(cb_ref[...], (SUBLANES, c))
    conv = functools.partial(_conformer_tile, u_ref, w8_ref, bias, lg_ref, lb_ref, ya_ref, n=n, tm=tm, sub=sub)

    @pl.when(step == 0)
    def _():
        conv(0, 0)

    o_ref[0] = _out_mlp_body(x_ref[0], ya_ref[cur], yb_ref[0], yc_ref[0], mod_ref, g2_ref, gf_ref,
                             wo_ref, w1_ref, w2_ref, final_norm)
    conv(1 - cur, (pl.program_id(1) + 1) % tiles_per_seq)


def _out_mlp(x, ya, yb, yc, mod, g2, gf, wo, w1, w2, layer, *, final_norm):
    b, n, d = x.shape
    tm = min(n, 512)
    per_batch = mod.shape[0] != 1
    tok = lambda a: pl.BlockSpec((1, tm, a.shape[-1]), lambda i, j: (i, j, 0))
    resident = lambda a: _layer_weight(a, layer, pipeline_mode=pl.Buffered(1))
    return pl.pallas_call(
        functools.partial(_out_mlp_kernel, final_norm=final_norm),
        grid=(b, n // tm),
        in_specs=[tok(x), tok(ya), tok(yb), tok(yc),
                  pl.BlockSpec((1, 6, d), (lambda i, j: (i, 0, 0)) if per_batch else (lambda i, j: (0, 0, 0))),
                  pl.BlockSpec((1, d), lambda i, j: (0, 0)), pl.BlockSpec((1, d), lambda i, j: (0, 0)),
                  resident(wo), resident(w1), resident(w2)],
        out_specs=tok(x),
        out_shape=jax.ShapeDtypeStruct((b, n, d), F32),
        compiler_params=_params("parallel", "parallel"),
        name="out_proj_mlp",
    )(x, ya, yb, yc, mod, g2, gf, wo, w1, w2)


def _conv_out_mlp(x, u, yb, yc, conv_w, conv_b, ln_g, ln_b, mod, g2, gf, wo, w1, w2, layer, *, final_norm):
    b, n, d = x.shape
    c = u.shape[-1]
    tm = min(n, 512)
    nt = n // tm
    assert tm % CONV_NORM_TB == 0 and CONV_NORM_TB % CONV_TB == 0
    tok = lambda a: pl.BlockSpec((1, tm, a.shape[-1]), lambda i, j: (i, j, 0))
    row = lambda width: pl.BlockSpec((1, width), lambda i, j: (0, 0))
    resident = lambda a: _layer_weight(a, layer, pipeline_mode=pl.Buffered(1))
    next_batch = lambda i, j: (jnp.minimum(i + (j + 1) // nt, b - 1), 0, 0)
    return pl.pallas_call(
        functools.partial(_conv_out_mlp_kernel, final_norm=final_norm, n=n, tm=tm),
        grid=(b, nt),
        in_specs=[tok(x), pl.BlockSpec((1, n, c), next_batch), tok(yb), tok(yc),
                  pl.BlockSpec(conv_w.shape, lambda i, j: (0, 0)), row(c), row(c), row(c),
                  pl.BlockSpec((1, 6, d), lambda i, j: (i, 0, 0)), row(d), row(d),
                  resident(wo), resident(w1), resident(w2)],
        out_specs=tok(x),
        out_shape=jax.ShapeDtypeStruct((b, n, d), F32),
        scratch_shapes=[pltpu.VMEM((2, tm, c), BF16), pltpu.VMEM((CONV_WIDTH, SUBLANES, c), F32)],
        compiler_params=_params("arbitrary", "arbitrary"),
        name="conv_out_proj_mlp",
    )(x, u, yb, yc, conv_w, conv_b.reshape(1, c), ln_g.reshape(1, c), ln_b.reshape(1, c), mod, g2, gf, wo, w1, w2)


def kernel(x, c, ctx, c_ctx, norm1_g, norm2_g, ada_w, ada_b, w_in, w_out, conv_w, conv_b, conv_ln_g, conv_ln_b,
           na_rpb, lru_conv_w, lru_conv_b, lru_wx, lru_bx, lru_wa, lru_ba, lru_lambda, mlp_w1, mlp_w2, final_g):
    depth = w_in.shape[0]
    bsz, _, d = x.shape
    conv_dim = conv_w.shape[-1]
    lru_dim = lru_conv_w.shape[-1]
    na_dim = NA_HEADS * NA_HEAD_DIM
    dims = dict(conv_dim=conv_dim, na_dim=na_dim, lru_dim=lru_dim)

    cond = jnp.concatenate([c, c_ctx[None, :]], axis=0)
    cond = jnp.pad(cond, ((0, -(bsz + 1) % SUBLANES), (0, 0)))
    mod_all = _modulation(cond, ada_w, ada_b)
    w_in_b, w_out_b = w_in.astype(BF16), w_out.astype(BF16)
    w1_b, w2_b = mlp_w1.astype(BF16), mlp_w2.astype(BF16)
    cx = ctx
    for l in range(depth):
        last = l == depth - 1
        mod = mod_all[l, :bsz].reshape(bsz, 6, d)
        mod_c = mod_all[l, bsz:bsz + 1].reshape(1, 6, d)
        g1 = norm1_g[l].reshape(1, d)

        lat = _in_proj(x, mod, g1, w_in_b, l, **dims)
        cxp = _in_proj(cx, mod_c, g1, w_in_b, l, **dims, want=("k", "v", "rx", "rg") if last else IN_PROJ_OUTPUTS)

        y_b = _na_latent(lat["q"], lat["k"], lat["v"], cxp["k"], cxp["v"], na_rpb[l], unroll=NA_ROW_UNROLL)
        y_c, yc_c = _rglru(lat["rx"], lat["rg"], cxp["rx"], cxp["rg"], lru_conv_w[l], lru_conv_b[l],
                           lru_wx[l], lru_bx[l], lru_wa[l], lru_ba[l], lru_lambda[l])

        g2 = norm2_g[l].reshape(1, d)
        gf = final_g.reshape(1, d)
        x = _conv_out_mlp(x, lat["u"], y_b, y_c, conv_w[l], conv_b[l], conv_ln_g[l], conv_ln_b[l], mod, g2, gf,
                          w_out_b, w1_b, w2_b, l, final_norm=last)
        if not last:
            yc_a = _conformer(cxp["u"], conv_w[l], conv_b[l], conv_ln_g[l], conv_ln_b[l])
            yc_b = _na_context(cxp["q"], cxp["k"], cxp["v"])
            cx = _out_mlp(cx, yc_a, yc_b, yc_c, mod_c, g2, gf, w_out_b, w1_b, w2_b, l, final_norm=False)
    return x
```

```python
import functools

import jax
import jax.numpy as jnp
from jax import lax
from jax.experimental import pallas as pl
from jax.experimental.pallas import tpu as pltpu

GRID_W = 64
CONV_WIDTH = 31
NA_HEADS = 8
NA_HEAD_DIM = 64
NA_ROWS = 8
NA_COLS = 16
LRU_BLOCKS = 4
LRU_CONV = 4
LRU_C = 8.0
EPS = 1e-6
NEG_INF = -1e30
LOG2_E = 1.4426950408889634

SUBLANES = 8
LANES = 128
HEADS_PER_STEP = 4
HEAD_LANES = HEADS_PER_STEP * NA_HEAD_DIM
VMEM_LIMIT_BYTES = 56 * 1024 * 1024

F32 = jnp.float32
BF16 = jnp.bfloat16


def _sigmoid(x):
    return 1.0 / (1.0 + jnp.exp(-x))


def _gelu_tanh(x):
    return 0.5 * x * (1.0 + jnp.tanh(0.7978845608028654 * (x + 0.044715 * (x * x * x))))


def _rms(x):
    return x * lax.rsqrt(jnp.mean(x * x, axis=-1, keepdims=True) + EPS)


def _layer_weight(w_all, layer, **kwargs):
    return pl.BlockSpec((None,) + w_all.shape[1:], lambda i, j: (layer, 0, 0), **kwargs)


def _params(*sem):
    return pltpu.CompilerParams(dimension_semantics=sem, vmem_limit_bytes=VMEM_LIMIT_BYTES)


def _mod_kernel(c_ref, w_ref, b_ref, o_ref):
    c = c_ref[...]
    s = c * _sigmoid(c)
    o_ref[0] = jnp.dot(s, w_ref[0], preferred_element_type=F32,
                       precision=lax.Precision.HIGHEST) + b_ref[0]


def _modulation(cc, ada_w, ada_b):
    depth, d, n = ada_w.shape
    r = cc.shape[0]
    tn = n // 4
    return pl.pallas_call(
        _mod_kernel,
        grid=(depth, n // tn),
        in_specs=[pl.BlockSpec((r, d), lambda l, j: (0, 0)),
                  pl.BlockSpec((1, d, tn), lambda l, j: (l, 0, j)),
                  pl.BlockSpec((1, 1, tn), lambda l, j: (l, 0, j))],
        out_specs=pl.BlockSpec((1, r, tn), lambda l, j: (l, 0, j)),
        out_shape=jax.ShapeDtypeStruct((depth, r, n), F32),
        compiler_params=_params("parallel", "parallel"),
        name="adaln_modulation",
    )(cc, ada_w, ada_b.reshape(depth, 1, n))


IN_PROJ_OUTPUTS = ("u", "q", "k", "v", "rx", "rg")


def _in_proj_kernel(x_ref, mod_ref, g_ref, w_ref, *out_refs, conv_dim, na_dim, lru_dim, want):
    out = dict(zip(want, out_refs))
    x = x_ref[0]
    h = _rms(x) * g_ref[...]
    h = h * (1.0 + mod_ref[0, 1:2, :]) + mod_ref[0, 0:1, :]
    hb = h.astype(BF16)

    def proj(off, width):
        return jnp.dot(hb, w_ref[:, off:off + width], preferred_element_type=F32)

    q_off = 2 * conv_dim
    if "u" in out:
        out["u"][0] = proj(0, conv_dim) * _sigmoid(proj(conv_dim, conv_dim))
    if "q" in out:
        out["q"][0] = (proj(q_off, na_dim) * (NA_HEAD_DIM ** -0.5 * LOG2_E)).astype(BF16)
    if "k" in out:
        out["k"][0] = proj(q_off + na_dim, na_dim).astype(BF16)
    if "v" in out:
        out["v"][0] = proj(q_off + 2 * na_dim, na_dim).astype(BF16)
    if "rx" in out:
        out["rx"][0] = proj(q_off + 3 * na_dim, lru_dim)
    if "rg" in out:
        out["rg"][0] = proj(q_off + 3 * na_dim + lru_dim, lru_dim)


IN_PROJ_TM = 1024


def _in_proj(x, mod, g, w_all, layer, *, conv_dim, na_dim, lru_dim, want=IN_PROJ_OUTPUTS):
    b, n, d = x.shape
    tm = min(n, IN_PROJ_TM)
    per_batch = mod.shape[0] != 1
    kinds = dict(u=(conv_dim, F32), q=(na_dim, BF16), k=(na_dim, BF16), v=(na_dim, BF16),
                 rx=(lru_dim, F32), rg=(lru_dim, F32))
    outs = pl.pallas_call(
        functools.partial(_in_proj_kernel, conv_dim=conv_dim, na_dim=na_dim, lru_dim=lru_dim, want=want),
        grid=(b, n // tm),
        in_specs=[pl.BlockSpec((1, tm, d), lambda i, j: (i, j, 0)),
                  pl.BlockSpec((1, 6, d), (lambda i, j: (i, 0, 0)) if per_batch else (lambda i, j: (0, 0, 0))),
                  pl.BlockSpec((1, d), lambda i, j: (0, 0)),
                  _layer_weight(w_all, layer)],
        out_specs=[pl.BlockSpec((1, tm, kinds[name][0]), lambda i, j: (i, j, 0)) for name in want],
        out_shape=[jax.ShapeDtypeStruct((b, n, kinds[name][0]), kinds[name][1]) for name in want],
        compiler_params=_params("parallel", "parallel"),
        name="in_proj",
    )(x, mod, g, w_all)
    return dict(zip(want, outs))


CONV_TB = 32
CONV_PAD = 16
CONV_NORM_TB = 128


def _shifted_tiles(tiles, off, count, sub):
    a, r = divmod(off, SUBLANES)
    if r == 0:
        return tiles[a:a + count]
    rolled = [pltpu.roll(t, SUBLANES - r, 0) for t in tiles[a:a + count + 1]]
    keep = sub < SUBLANES - r
    return [jnp.where(keep, rolled[i], rolled[i + 1]) for i in range(count)]


def _conformer_kernel(u_ref, w_ref, b_ref, lg_ref, lb_ref, o_ref, pad_ref, w8_ref, conv_ref, *, n):
    c = u_ref.shape[-1]
    zeros = jnp.zeros((CONV_PAD, c), F32)
    pad_ref[0:CONV_PAD, :] = zeros
    pad_ref[n + CONV_PAD:n + 2 * CONV_PAD, :] = zeros
    pad_ref[CONV_PAD:n + CONV_PAD, :] = u_ref[0]
    for k in range(CONV_WIDTH):
        w8_ref[k] = jnp.broadcast_to(w_ref[k:k + 1, :], (SUBLANES, c))
    sub = lax.broadcasted_iota(jnp.int32, (SUBLANES, c), 0)
    bias = jnp.broadcast_to(b_ref[...], (SUBLANES, c))
    half = CONV_WIDTH // 2
    out_tiles = CONV_TB // SUBLANES
    win_tiles = (CONV_TB + 2 * CONV_PAD) // SUBLANES

    def block(i, carry):
        start = pl.multiple_of(i * CONV_TB, CONV_TB)
        tiles = [pad_ref[pl.ds(start + SUBLANES * m, SUBLANES), :] for m in range(win_tiles)]
        acc = [bias] * out_tiles
        for r in range(SUBLANES):
            sh = _shifted_tiles(tiles, r, win_tiles - 1, sub)
            for a in range(win_tiles - out_tiles):
                k = r + SUBLANES * a - (CONV_PAD - half)
                if 0 <= k < CONV_WIDTH:
                    w = w8_ref[k]
                    acc = [acc[j] + w * sh[j + a] for j in range(out_tiles)]
        conv_ref[pl.ds(start, CONV_TB), :] = jnp.concatenate(acc, axis=0)
        return carry

    lax.fori_loop(0, n // CONV_TB, block, 0)

    def norm_block(i, carry):
        start = pl.multiple_of(i * CONV_NORM_TB, CONV_NORM_TB)
        conv = conv_ref[pl.ds(start, CONV_NORM_TB), :]
        mu = jnp.mean(conv, axis=-1, keepdims=True)
        cen = conv - mu
        var = jnp.mean(cen * cen, axis=-1, keepdims=True)
        y = cen * lax.rsqrt(var + EPS) * lg_ref[...] + lb_ref[...]
        o_ref[0, pl.ds(start, CONV_NORM_TB), :] = (y * _sigmoid(y)).astype(o_ref.dtype)
        return carry

    lax.fori_loop(0, n // CONV_NORM_TB, norm_block, 0, unroll=4)


def _conformer(u, w, b, ln_g, ln_b):
    bsz, n, c = u.shape
    row = pl.BlockSpec((1, c), lambda i: (0, 0))
    return pl.pallas_call(
        functools.partial(_conformer_kernel, n=n),
        grid=(bsz,),
        in_specs=[pl.BlockSpec((1, n, c), lambda i: (i, 0, 0)),
                  pl.BlockSpec(w.shape, lambda i: (0, 0)), row, row, row],
        out_specs=pl.BlockSpec((1, n, c), lambda i: (i, 0, 0)),
        out_shape=jax.ShapeDtypeStruct((bsz, n, c), BF16),
        scratch_shapes=[pltpu.VMEM((n + 2 * CONV_PAD, c), F32), pltpu.VMEM((CONV_WIDTH, SUBLANES, c), F32),
                        pltpu.VMEM((n, c), F32)],
        compiler_params=_params("parallel"),
        name="conformer_conv",
    )(u, w, b.reshape(1, c), ln_g.reshape(1, c), ln_b.reshape(1, c))


def _head_of_lane():
    return lax.broadcasted_iota(jnp.int32, (1, HEAD_LANES), 1) // NA_HEAD_DIM


def _stack_heads(qr, head):
    zero = jnp.zeros_like(qr)
    return jnp.concatenate([jnp.where(head == h, qr, zero) for h in range(HEADS_PER_STEP)], axis=0)


def _unstack_heads(o, head, m):
    out = o[0:m]
    for h in range(1, HEADS_PER_STEP):
        out = jnp.where(head == h, o[h * m:(h + 1) * m], out)
    return out


def _build_na_bias(rpb_ref, bias_ref, group):
    qi = lax.broadcasted_iota(jnp.int32, (GRID_W, GRID_W), 0)
    ki = lax.broadcasted_iota(jnp.int32, (GRID_W, GRID_W), 1)
    rel = ki - qi
    col_start = jnp.clip(qi - NA_COLS // 2, 0, GRID_W - NA_COLS)
    valid = (ki >= col_start) & (ki < col_start + NA_COLS)
    n_r = 2 * NA_ROWS - 1
    n_c = 2 * NA_COLS - 1

    def per_head(h, carry):
        q0 = pl.multiple_of(h * GRID_W, GRID_W)
        for ri in range(n_r):
            base = ((group * HEADS_PER_STEP + h) * n_r + ri) * n_c
            t = jnp.full((GRID_W, GRID_W), NEG_INF, F32)
            for dc in range(n_c):
                t = jnp.where(rel == dc - (NA_COLS - 1), rpb_ref[base + dc] * LOG2_E, t)
            t = jnp.where(valid, t, NEG_INF)
            for d in range(NA_ROWS):
                j = ri + d - (NA_ROWS - 1)
                if 0 <= j < NA_ROWS:
                    bias_ref[d, pl.ds(q0, GRID_W), j * GRID_W:(j + 1) * GRID_W] = t
        return carry

    lax.fori_loop(0, HEADS_PER_STEP, per_head, 0)


def _lane_reduce(tiles, combine, reduce):
    by_width = {}
    for t in tiles:
        w = t.shape[-1]
        by_width[w] = t if w not in by_width else combine(by_width[w], t)
    parts = [reduce(t, axis=-1, keepdims=True) for t in by_width.values()]
    out = parts[0]
    for part in parts[1:]:
        out = combine(out, part)
    return out


def _na_latent_kernel(rpb_ref, q_ref, k_ref, v_ref, kc_ref, vc_ref, o_ref, bias_ref, *, rows, unroll):
    @pl.when(pl.program_id(1) == 0)
    def _():
        _build_na_bias(rpb_ref, bias_ref, pl.program_id(0))

    head = _head_of_lane()
    band = NA_ROWS * GRID_W
    nt = (((1,), (1,)), ((), ()))
    bounds = list(range(0, band, HEAD_LANES)) + [band]

    def row_step(r, carry):
        rs = jnp.clip(r - NA_ROWS // 2, 0, rows - NA_ROWS)
        q0 = pl.multiple_of(r * GRID_W, GRID_W)
        k0 = pl.multiple_of(rs * GRID_W, GRID_W)
        qs = _stack_heads(q_ref[0, pl.ds(q0, GRID_W), :], head)
        s_loc = lax.dot_general(qs, k_ref[0, pl.ds(k0, band), :], nt, preferred_element_type=F32) + bias_ref[r - rs]
        s_ctx = lax.dot_general(qs, kc_ref[0], nt, preferred_element_type=F32)
        tiles = [s_loc[:, lo:hi] for lo, hi in zip(bounds[:-1], bounds[1:])] + [s_ctx]
        m = _lane_reduce(tiles, jnp.maximum, jnp.max)
        probs = [jnp.exp2(t - m) for t in tiles]
        inv = 1.0 / _lane_reduce(probs, jnp.add, jnp.sum)
        o = jnp.dot(jnp.concatenate(probs[:-1], axis=-1).astype(BF16), v_ref[0, pl.ds(k0, band), :],
                    preferred_element_type=F32)
        o = (o + jnp.dot(probs[-1].astype(BF16), vc_ref[0], preferred_element_type=F32)) * inv
        o_ref[0, pl.ds(q0, GRID_W), :] = _unstack_heads(o, head, GRID_W).astype(o_ref.dtype)
        return carry

    lax.fori_loop(0, rows, row_step, 0, unroll=unroll)


NA_ROW_UNROLL = 32


def _na_latent(q, k, v, kc, vc, rpb, *, unroll):
    bsz, n, na_dim = q.shape
    nc = kc.shape[1]
    groups = na_dim // HEAD_LANES
    rows = n // GRID_W
    unroll = min(unroll, rows)
    assert rows >= NA_ROWS and rows % unroll == 0
    lat = pl.BlockSpec((1, n, HEAD_LANES), lambda g, i: (i, 0, g))
    ctx = pl.BlockSpec((1, nc, HEAD_LANES), lambda g, i: (i, 0, g))
    return pl.pallas_call(
        functools.partial(_na_latent_kernel, rows=rows, unroll=unroll),
        grid=(groups, bsz),
        in_specs=[pl.BlockSpec(memory_space=pltpu.SMEM), lat, lat, lat, ctx, ctx],
        out_specs=lat,
        out_shape=jax.ShapeDtypeStruct((bsz, n, na_dim), BF16),
        scratch_shapes=[pltpu.VMEM((NA_ROWS, HEADS_PER_STEP * GRID_W, NA_ROWS * GRID_W), F32)],
        compiler_params=_params("arbitrary", "arbitrary"),
        name="na_latent",
    )(rpb.reshape(-1), q, k, v, kc, vc)


def _na_context_kernel(q_ref, k_ref, v_ref, o_ref):
    head = _head_of_lane()
    n = q_ref.shape[1]
    qs = _stack_heads(q_ref[0], head)
    s = lax.dot_general(qs, k_ref[0], (((1,), (1,)), ((), ())), preferred_element_type=F32)
    p = jnp.exp2(s - jnp.max(s, axis=-1, keepdims=True))
    denom = jnp.sum(p, axis=-1, keepdims=True)
    o = jnp.dot(p.astype(BF16), v_ref[0], preferred_element_type=F32) * (1.0 / denom)
    o_ref[0] = _unstack_heads(o, head, n).astype(o_ref.dtype)


def _na_context(q, k, v):
    bsz, n, na_dim = q.shape
    blk = pl.BlockSpec((1, n, HEAD_LANES), lambda i, g: (i, 0, g))
    return pl.pallas_call(
        _na_context_kernel,
        grid=(bsz, na_dim // HEAD_LANES),
        in_specs=[blk, blk, blk],
        out_specs=blk,
        out_shape=jax.ShapeDtypeStruct((bsz, n, na_dim), BF16),
        compiler_params=_params("parallel", "parallel"),
        name="na_context",
    )(q, k, v)


LRU_TB = 2048
LRU_SCAN_UNROLL = 16
LRU_PAD = 8


def _rglru_kernel(xl_ref, gl_ref, xc_ref, gc_ref, cw_ref, vec_ref, wx_ref, wa_ref, yl_ref, yc_ref,
                  padl_ref, padc_ref, a_ref, b_ref, h_ref, *, n_lat, n_ctx):
    c = xl_ref.shape[-1]
    zeros = jnp.zeros((LRU_PAD, c), F32)
    for pad_ref, src_ref, n in ((padc_ref, xc_ref, n_ctx), (padl_ref, xl_ref, n_lat)):
        pad_ref[0:LRU_PAD, :] = zeros
        pad_ref[n + LRU_PAD:n + 2 * LRU_PAD, :] = zeros
        pad_ref[LRU_PAD:n + LRU_PAD, :] = src_ref[0]
    sub = lax.broadcasted_iota(jnp.int32, (SUBLANES, c), 0)

    for direction in range(2):
        reverse = direction == 1
        conv_b = vec_ref[direction, 0:1, :]
        bx = vec_ref[direction, 1:2, :]
        ba = vec_ref[direction, 2:3, :]
        neg_lam = -vec_ref[direction, 3:4, :]
        softplus = jnp.maximum(neg_lam, 0.0) + jnp.log(1.0 + jnp.exp(-jnp.abs(neg_lam)))
        decay = -LRU_C * softplus

        conv_b8 = jnp.broadcast_to(conv_b, (SUBLANES, c))
        cw8 = [jnp.broadcast_to(cw_ref[direction, k:k + 1, :], (SUBLANES, c)) for k in range(LRU_CONV)]

        def coeffs(pad_ref, base, n):
            tb = min(n, LRU_TB)
            nt = tb // SUBLANES

            def block(i, carry):
                start = pl.multiple_of(i * tb, tb)
                tiles = [pad_ref[pl.ds(start + SUBLANES * m, SUBLANES), :]
                         for m in range(nt + 2 * LRU_PAD // SUBLANES)]
                u_tiles = [conv_b8] * nt
                for k in range(LRU_CONV):
                    off = LRU_PAD + k - (0 if reverse else LRU_CONV - 1)
                    sh = _shifted_tiles(tiles, off, nt, sub)
                    u_tiles = [u_tiles[j] + cw8[k] * sh[j] for j in range(nt)]
                u = jnp.concatenate(u_tiles, axis=0)
                ub = u.astype(BF16)
                gx = 1.0 / (1.0 + jnp.exp(jnp.dot(ub, wx_ref[direction], preferred_element_type=F32) - bx))
                ga = 1.0 / (1.0 + jnp.exp(jnp.dot(ub, wa_ref[direction], preferred_element_type=F32) - ba))
                a = jnp.exp(decay * ga)
                rem = 1.0 - a * a
                coef = jnp.where(rem > 0.0, rem * lax.rsqrt(rem), 0.0)
                a_ref[pl.ds(base + start, tb), :] = a
                b_ref[pl.ds(base + start, tb), :] = coef * (gx * u)
                return carry

            lax.fori_loop(0, n // tb, block, 0)

        coeffs(padc_ref, 0, n_ctx)
        coeffs(padl_ref, n_ctx, n_lat)

        def scan(base, n, hb):
            chunks = n // SUBLANES

            def chunk(j, hb):
                jj = chunks - 1 - j if reverse else j
                rows = pl.ds(pl.multiple_of(base + jj * SUBLANES, SUBLANES), SUBLANES)
                a = a_ref[rows, :]
                b = b_ref[rows, :]
                for d in (1, 2, 4):
                    shift = SUBLANES - d if reverse else d
                    keep = (sub < SUBLANES - d) if reverse else (sub >= d)
                    ra = pltpu.roll(a, shift, 0)
                    rb = pltpu.roll(b, shift, 0)
                    b = jnp.where(keep, a * rb + b, b)
                    a = jnp.where(keep, a * ra, a)
                h = a * hb + b
                if reverse:
                    h_ref[rows, :] = h_ref[rows, :] + h
                else:
                    h_ref[rows, :] = h
                last = 0 if reverse else SUBLANES - 1
                a_last = jnp.broadcast_to(a[last:last + 1, :], (SUBLANES, c))
                b_last = jnp.broadcast_to(b[last:last + 1, :], (SUBLANES, c))
                return a_last * hb + b_last

            return lax.fori_loop(0, chunks, chunk, hb, unroll=LRU_SCAN_UNROLL)

        hb = scan(0, n_ctx, jnp.zeros((SUBLANES, c), F32))
        scan(n_ctx, n_lat, hb)

    def gate_out(g_ref, y_ref, base, n):
        tb = min(n, LRU_TB)

        def block(i, carry):
            start = pl.multiple_of(i * tb, tb)
            g = g_ref[0, pl.ds(start, tb), :]
            y_ref[0, pl.ds(start, tb), :] = (_gelu_tanh(g) * h_ref[pl.ds(base + start, tb), :]).astype(y_ref.dtype)
            return carry

        lax.fori_loop(0, n // tb, block, 0)

    gate_out(gc_ref, yc_ref, 0, n_ctx)
    gate_out(gl_ref, yl_ref, n_ctx, n_lat)


def _block_diag(w):
    two, nb, m, _ = w.shape
    eye = jnp.eye(nb, dtype=w.dtype)
    return (w[:, :, :, None, :] * eye[None, :, None, :, None]).reshape(two, nb * m, nb * m)


def _rglru(xl, gl, xc, gc, conv_w, conv_b, wx, bx, wa, ba, lam):
    bsz, n_lat, c = xl.shape
    n_ctx = xc.shape[1]
    vec = jnp.stack([conv_b, bx, ba, lam], axis=1)
    wxd = _block_diag(-wx).astype(BF16)
    wad = _block_diag(-wa).astype(BF16)
    lat = pl.BlockSpec((1, n_lat, c), lambda i: (i, 0, 0))
    ctx = pl.BlockSpec((1, n_ctx, c), lambda i: (i, 0, 0))
    full = lambda a: pl.BlockSpec(a.shape, lambda i: (0,) * a.ndim)
    return pl.pallas_call(
        functools.partial(_rglru_kernel, n_lat=n_lat, n_ctx=n_ctx),
        grid=(bsz,),
        in_specs=[lat, lat, ctx, ctx, full(conv_w), full(vec), full(wxd), full(wad)],
        out_specs=[lat, ctx],
        out_shape=[jax.ShapeDtypeStruct((bsz, n_lat, c), BF16), jax.ShapeDtypeStruct((bsz, n_ctx, c), BF16)],
        scratch_shapes=[pltpu.VMEM((n_lat + 2 * LRU_PAD, c), F32), pltpu.VMEM((n_ctx + 2 * LRU_PAD, c), F32),
                        pltpu.VMEM((n_ctx + n_lat, c), F32), pltpu.VMEM((n_ctx + n_lat, c), F32),
                        pltpu.VMEM((n_ctx + n_lat, c), F32)],
        compiler_params=_params("parallel"),
        name="rglru",
    )(xl, gl, xc, gc, conv_w, vec, wxd, wad)


FF_CHUNK = 1024


def _out_mlp_body(x, ya, yb, yc, mod_ref, g2_ref, gf_ref, wo_ref, w1_ref, w2_ref, final_norm):
    ca, cb, cc = ya.shape[-1], yb.shape[-1], yc.shape[-1]
    mix = (jnp.dot(ya, wo_ref[0:ca, :], preferred_element_type=F32)
           + jnp.dot(yb, wo_ref[ca:ca + cb, :], preferred_element_type=F32)
           + jnp.dot(yc, wo_ref[ca + cb:ca + cb + cc, :], preferred_element_type=F32))
    x1 = x + mod_ref[0, 2:3, :] * mix
    h2 = _rms(x1) * g2_ref[...]
    h2 = (h2 * (1.0 + mod_ref[0, 4:5, :]) + mod_ref[0, 3:4, :]).astype(BF16)
    acc = jnp.zeros(x1.shape, F32)
    for j in range(w1_ref.shape[1] // FF_CHUNK):
        cols = slice(j * FF_CHUNK, (j + 1) * FF_CHUNK)
        hid = jnp.maximum(jnp.dot(h2, w1_ref[:, cols], preferred_element_type=F32), 0.0)
        acc = acc + jnp.dot((hid * hid).astype(BF16), w2_ref[cols, :], preferred_element_type=F32)
    out = x1 + mod_ref[0, 5:6, :] * acc
    if final_norm:
        out = _rms(out) * gf_ref[...]
    return out


def _out_mlp_kernel(x_ref, ya_ref, yb_ref, yc_ref, mod_ref, g2_ref, gf_ref, wo_ref, w1_ref, w2_ref, o_ref, *,
                    final_norm):
    o_ref[0] = _out_mlp_body(x_ref[0], ya_ref[0], yb_ref[0], yc_ref[0], mod_ref, g2_ref, gf_ref,
                             wo_ref, w1_ref, w2_ref, final_norm)


def _conformer_tile(u_ref, w8_ref, bias, lg_ref, lb_ref, ya_ref, slot, tile, *, n, tm, sub):
    half = CONV_WIDTH // 2
    out_tiles = CONV_TB // SUBLANES
    win_tiles = (CONV_TB + 2 * CONV_PAD) // SUBLANES
    pad_tiles = CONV_PAD // SUBLANES
    blocks = tm // CONV_TB
    first = tile == 0
    last = tile == n // tm - 1
    zero = jnp.zeros((SUBLANES, u_ref.shape[-1]), F32)
    done = []
    for blk in range(blocks):
        base = tile * tm + (CONV_TB * blk - CONV_PAD)
        tiles = []
        for m in range(win_tiles):
            start = base + SUBLANES * m
            if blk == 0 and m < pad_tiles:
                t = u_ref[0, pl.ds(pl.multiple_of(jnp.maximum(start, 0), SUBLANES), SUBLANES), :]
                t = jnp.where(first, zero, t)
            elif blk == blocks - 1 and m >= win_tiles - pad_tiles:
                t = u_ref[0, pl.ds(pl.multiple_of(jnp.minimum(start, n - SUBLANES), SUBLANES), SUBLANES), :]
                t = jnp.where(last, zero, t)
            else:
                t = u_ref[0, pl.ds(pl.multiple_of(start, SUBLANES), SUBLANES), :]
            tiles.append(t)
        acc = [bias] * out_tiles
        for r in range(SUBLANES):
            sh = _shifted_tiles(tiles, r, win_tiles - 1, sub)
            for a in range(win_tiles - out_tiles):
                k = r + SUBLANES * a - (CONV_PAD - half)
                if 0 <= k < CONV_WIDTH:
                    w = w8_ref[k]
                    acc = [acc[j] + w * sh[j + a] for j in range(out_tiles)]
        done.extend(acc)
        if len(done) == CONV_NORM_TB // SUBLANES:
            conv = jnp.concatenate(done, axis=0)
            done = []
            mu = jnp.mean(conv, axis=-1, keepdims=True)
            cen = conv - mu
            var = jnp.mean(cen * cen, axis=-1, keepdims=True)
            y = cen * lax.rsqrt(var + EPS) * lg_ref[...] + lb_ref[...]
            row0 = CONV_TB * (blk + 1) - CONV_NORM_TB
            ya_ref[slot, row0:row0 + CONV_NORM_TB, :] = (y * _sigmoid(y)).astype(ya_ref.dtype)


def _conv_out_mlp_kernel(x_ref, u_ref, yb_ref, yc_ref, cw_ref, cb_ref, lg_ref, lb_ref, mod_ref, g2_ref, gf_ref,
                         wo_ref, w1_ref, w2_ref, o_ref, ya_ref, w8_ref, *, final_norm, n, tm):
    c = u_ref.shape[-1]
    tiles_per_seq = n // tm
    step = pl.program_id(0) * tiles_per_seq + pl.program_id(1)
    cur = step % 2
    for k in range(CONV_WIDTH):
        w8_ref[k] = jnp.broadcast_to(cw_ref[k:k + 1, :], (SUBLANES, c))
    sub = lax.broadcasted_iota(jnp.int32, (SUBLANES, c), 0)
    bias = jnp.broadcast_to(cb_ref[...], (SUBLANES, c))
    conv = functools.partial(_conformer_tile, u_ref, w8_ref, bias, lg_ref, lb_ref, ya_ref, n=n, tm=tm, sub=sub)

    @pl.when(step == 0)
    def _():
        conv(0, 0)

    o_ref[0] = _out_mlp_body(x_ref[0], ya_ref[cur], yb_ref[0], yc_ref[0], mod_ref, g2_ref, gf_ref,
                             wo_ref, w1_ref, w2_ref, final_norm)
    conv(1 - cur, (pl.program_id(1) + 1) % tiles_per_seq)


def _out_mlp(x, ya, yb, yc, mod, g2, gf, wo, w1, w2, layer, *, final_norm):
    b, n, d = x.shape
    tm = min(n, 512)
    per_batch = mod.shape[0] != 1
    tok = lambda a: pl.BlockSpec((1, tm, a.shape[-1]), lambda i, j: (i, j, 0))
    resident = lambda a: _layer_weight(a, layer, pipeline_mode=pl.Buffered(1))
    return pl.pallas_call(
        functools.partial(_out_mlp_kernel, final_norm=final_norm),
        grid=(b, n // tm),
        in_specs=[tok(x), tok(ya), tok(yb), tok(yc),
                  pl.BlockSpec((1, 6, d), (lambda i, j: (i, 0, 0)) if per_batch else (lambda i, j: (0, 0, 0))),
                  pl.BlockSpec((1, d), lambda i, j: (0, 0)), pl.BlockSpec((1, d), lambda i, j: (0, 0)),
                  resident(wo), resident(w1), resident(w2)],
        out_specs=tok(x),
        out_shape=jax.ShapeDtypeStruct((b, n, d), F32),
        compiler_params=_params("parallel", "parallel"),
        name="out_proj_mlp",
    )(x, ya, yb, yc, mod, g2, gf, wo, w1, w2)


def _conv_out_mlp(x, u, yb, yc, conv_w, conv_b, ln_g, ln_b, mod, g2, gf, wo, w1, w2, layer, *, final_norm):
    b, n, d = x.shape
    c = u.shape[-1]
    tm = min(n, 512)
    nt = n // tm
    assert tm % CONV_NORM_TB == 0 and CONV_NORM_TB % CONV_TB == 0
    tok = lambda a: pl.BlockSpec((1, tm, a.shape[-1]), lambda i, j: (i, j, 0))
    row = lambda width: pl.BlockSpec((1, width), lambda i, j: (0, 0))
    resident = lambda a: _layer_weight(a, layer, pipeline_mode=pl.Buffered(1))
    next_batch = lambda i, j: (jnp.minimum(i + (j + 1) // nt, b - 1), 0, 0)
    return pl.pallas_call(
        functools.partial(_conv_out_mlp_kernel, final_norm=final_norm, n=n, tm=tm),
        grid=(b, nt),
        in_specs=[tok(x), pl.BlockSpec((1, n, c), next_batch), tok(yb), tok(yc),
                  pl.BlockSpec(conv_w.shape, lambda i, j: (0, 0)), row(c), row(c), row(c),
                  pl.BlockSpec((1, 6, d), lambda i, j: (i, 0, 0)), row(d), row(d),
                  resident(wo), resident(w1), resident(w2)],
        out_specs=tok(x),
        out_shape=jax.ShapeDtypeStruct((b, n, d), F32),
        scratch_shapes=[pltpu.VMEM((2, tm, c), BF16), pltpu.VMEM((CONV_WIDTH, SUBLANES, c), F32)],
        compiler_params=_params("arbitrary", "arbitrary"),
        name="conv_out_proj_mlp",
    )(x, u, yb, yc, conv_w, conv_b.reshape(1, c), ln_g.reshape(1, c), ln_b.reshape(1, c), mod, g2, gf, wo, w1, w2)


def kernel(x, c, ctx, c_ctx, norm1_g, norm2_g, ada_w, ada_b, w_in, w_out, conv_w, conv_b, conv_ln_g, conv_ln_b,
           na_rpb, lru_conv_w, lru_conv_b, lru_wx, lru_bx, lru_wa, lru_ba, lru_lambda, mlp_w1, mlp_w2, final_g):
    depth = w_in.shape[0]
    bsz, _, d = x.shape
    conv_dim = conv_w.shape[-1]
    lru_dim = lru_conv_w.shape[-1]
    na_dim = NA_HEADS * NA_HEAD_DIM
    dims = dict(conv_dim=conv_dim, na_dim=na_dim, lru_dim=lru_dim)

    cond = jnp.concatenate([c, c_ctx[None, :]], axis=0)
    cond = jnp.pad(cond, ((0, -(bsz + 1) % SUBLANES), (0, 0)))
    mod_all = _modulation(cond, ada_w, ada_b)
    w_in_b, w_out_b = w_in.astype(BF16), w_out.astype(BF16)
    w1_b, w2_b = mlp_w1.astype(BF16), mlp_w2.astype(BF16)
    cx = ctx
    for l in range(depth):
        last = l == depth - 1
        mod = mod_all[l, :bsz].reshape(bsz, 6, d)
        mod_c = mod_all[l, bsz:bsz + 1].reshape(1, 6, d)
        g1 = norm1_g[l].reshape(1, d)

        lat = _in_proj(x, mod, g1, w_in_b, l, **dims)
        cxp = _in_proj(cx, mod_c, g1, w_in_b, l, **dims, want=("k", "v", "rx", "rg") if last else IN_PROJ_OUTPUTS)

        y_b = _na_latent(lat["q"], lat["k"], lat["v"], cxp["k"], cxp["v"], na_rpb[l], unroll=NA_ROW_UNROLL)
        y_c, yc_c = _rglru(lat["rx"], lat["rg"], cxp["rx"], cxp["rg"], lru_conv_w[l], lru_conv_b[l],
                           lru_wx[l], lru_bx[l], lru_wa[l], lru_ba[l], lru_lambda[l])

        g2 = norm2_g[l].reshape(1, d)
        gf = final_g.reshape(1, d)
        x = _conv_out_mlp(x, lat["u"], y_b, y_c, conv_w[l], conv_b[l], conv_ln_g[l], conv_ln_b[l], mod, g2, gf,
                          w_out_b, w1_b, w2_b, l, final_norm=last)
        if not last:
            yc_a = _conformer(cxp["u"], conv_w[l], conv_b[l], conv_ln_g[l], conv_ln_b[l])
            yc_b = _na_context(cxp["q"], cxp["k"], cxp["v"])
            cx = _out_mlp(cx, yc_a, yc_b, yc_c, mod_c, g2, gf, w_out_b, w1_b, w2_b, l, final_norm=False)
    return x
```

```python
import functools

import jax
import jax.numpy as jnp
from jax import lax
from jax.experimental import pallas as pl
from jax.experimental.pallas import tpu as pltpu

GRID_W = 64
CONV_WIDTH = 31
NA_HEADS = 8
NA_HEAD_DIM = 64
NA_ROWS = 8
NA_COLS = 16
LRU_BLOCKS = 4
LRU_CONV = 4
LRU_C = 8.0
EPS = 1e-6
NEG_INF = -1e30
LOG2_E = 1.4426950408889634

SUBLANES = 8
LANES = 128
HEADS_PER_STEP = 4
HEAD_LANES = HEADS_PER_STEP * NA_HEAD_DIM
VMEM_LIMIT_BYTES = 56 * 1024 * 1024

F32 = jnp.float32
BF16 = jnp.bfloat16


def _sigmoid(x):
    return 1.0 / (1.0 + jnp.exp(-x))


def _gelu_tanh(x):
    return 0.5 * x * (1.0 + jnp.tanh(0.7978845608028654 * (x + 0.044715 * (x * x * x))))


def _rms(x):
    return x * lax.rsqrt(jnp.mean(x * x, axis=-1, keepdims=True) + EPS)


def _layer_weight(w_all, layer, **kwargs):
    return pl.BlockSpec((None,) + w_all.shape[1:], lambda i, j: (layer, 0, 0), **kwargs)


def _params(*sem):
    return pltpu.CompilerParams(dimension_semantics=sem, vmem_limit_bytes=VMEM_LIMIT_BYTES)


def _mod_kernel(c_ref, w_ref, b_ref, o_ref):
    c = c_ref[...]
    s = c * _sigmoid(c)
    o_ref[0] = jnp.dot(s, w_ref[0], preferred_element_type=F32,
                       precision=lax.Precision.HIGHEST) + b_ref[0]


def _modulation(cc, ada_w, ada_b):
    depth, d, n = ada_w.shape
    r = cc.shape[0]
    tn = n // 4
    return pl.pallas_call(
        _mod_kernel,
        grid=(depth, n // tn),
        in_specs=[pl.BlockSpec((r, d), lambda l, j: (0, 0)),
                  pl.BlockSpec((1, d, tn), lambda l, j: (l, 0, j)),
                  pl.BlockSpec((1, 1, tn), lambda l, j: (l, 0, j))],
        out_specs=pl.BlockSpec((1, r, tn), lambda l, j: (l, 0, j)),
        out_shape=jax.ShapeDtypeStruct((depth, r, n), F32),
        compiler_params=_params("parallel", "parallel"),
        name="adaln_modulation",
    )(cc, ada_w, ada_b.reshape(depth, 1, n))


IN_PROJ_OUTPUTS = ("u", "q", "k", "v", "rx", "rg")


def _in_proj_kernel(x_ref, mod_ref, g_ref, w_ref, *out_refs, conv_dim, na_dim, lru_dim, want):
    out = dict(zip(want, out_refs))
    x = x_ref[0]
    h = _rms(x) * g_ref[...]
    h = h * (1.0 + mod_ref[0, 1:2, :]) + mod_ref[0, 0:1, :]
    hb = h.astype(BF16)

    def proj(off, width):
        return jnp.dot(hb, w_ref[:, off:off + width], preferred_element_type=F32)

    q_off = 2 * conv_dim
    if "u" in out:
        out["u"][0] = proj(0, conv_dim) * _sigmoid(proj(conv_dim, conv_dim))
    if "q" in out:
        out["q"][0] = (proj(q_off, na_dim) * (NA_HEAD_DIM ** -0.5 * LOG2_E)).astype(BF16)
    if "k" in out:
        out["k"][0] = proj(q_off + na_dim, na_dim).astype(BF16)
    if "v" in out:
        out["v"][0] = proj(q_off + 2 * na_dim, na_dim).astype(BF16)
    if "rx" in out:
        out["rx"][0] = proj(q_off + 3 * na_dim, lru_dim)
    if "rg" in out:
        out["rg"][0] = proj(q_off + 3 * na_dim + lru_dim, lru_dim)


IN_PROJ_TM = 1024


def _in_proj(x, mod, g, w_all, layer, *, conv_dim, na_dim, lru_dim, want=IN_PROJ_OUTPUTS):
    b, n, d = x.shape
    tm = min(n, IN_PROJ_TM)
    per_batch = mod.shape[0] != 1
    kinds = dict(u=(conv_dim, F32), q=(na_dim, BF16), k=(na_dim, BF16), v=(na_dim, BF16),
                 rx=(lru_dim, F32), rg=(lru_dim, F32))
    outs = pl.pallas_call(
        functools.partial(_in_proj_kernel, conv_dim=conv_dim, na_dim=na_dim, lru_dim=lru_dim, want=want),
        grid=(b, n // tm),
        in_specs=[pl.BlockSpec((1, tm, d), lambda i, j: (i, j, 0)),
                  pl.BlockSpec((1, 6, d), (lambda i, j: (i, 0, 0)) if per_batch else (lambda i, j: (0, 0, 0))),
                  pl.BlockSpec((1, d), lambda i, j: (0, 0)),
                  _layer_weight(w_all, layer)],
        out_specs=[pl.BlockSpec((1, tm, kinds[name][0]), lambda i, j: (i, j, 0)) for name in want],
        out_shape=[jax.ShapeDtypeStruct((b, n, kinds[name][0]), kinds[name][1]) for name in want],
        compiler_params=_params("parallel", "parallel"),
        name="in_proj",
    )(x, mod, g, w_all)
    return dict(zip(want, outs))


CONV_TB = 32
CONV_PAD = 16
CONV_NORM_TB = 128


def _shifted_tiles(tiles, off, count, sub):
    a, r = divmod(off, SUBLANES)
    if r == 0:
        return tiles[a:a + count]
    rolled = [pltpu.roll(t, SUBLANES - r, 0) for t in tiles[a:a + count + 1]]
    keep = sub < SUBLANES - r
    return [jnp.where(keep, rolled[i], rolled[i + 1]) for i in range(count)]


def _conformer_kernel(u_ref, w_ref, b_ref, lg_ref, lb_ref, o_ref, pad_ref, w8_ref, conv_ref, *, n):
    c = u_ref.shape[-1]
    zeros = jnp.zeros((CONV_PAD, c), F32)
    pad_ref[0:CONV_PAD, :] = zeros
    pad_ref[n + CONV_PAD:n + 2 * CONV_PAD, :] = zeros
    pad_ref[CONV_PAD:n + CONV_PAD, :] = u_ref[0]
    for k in range(CONV_WIDTH):
        w8_ref[k] = jnp.broadcast_to(w_ref[k:k + 1, :], (SUBLANES, c))
    sub = lax.broadcasted_iota(jnp.int32, (SUBLANES, c), 0)
    bias = jnp.broadcast_to(b_ref[...], (SUBLANES, c))
    half = CONV_WIDTH // 2
    out_tiles = CONV_TB // SUBLANES
    win_tiles = (CONV_TB + 2 * CONV_PAD) // SUBLANES

    def block(i, carry):
        start = pl.multiple_of(i * CONV_TB, CONV_TB)
        tiles = [pad_ref[pl.ds(start + SUBLANES * m, SUBLANES), :] for m in range(win_tiles)]
        acc = [bias] * out_tiles
        for r in range(SUBLANES):
            sh = _shifted_tiles(tiles, r, win_tiles - 1, sub)
            for a in range(win_tiles - out_tiles):
                k = r + SUBLANES * a - (CONV_PAD - half)
                if 0 <= k < CONV_WIDTH:
                    w = w8_ref[k]
                    acc = [acc[j] + w * sh[j + a] for j in range(out_tiles)]
        conv_ref[pl.ds(start, CONV_TB), :] = jnp.concatenate(acc, axis=0)
        return carry

    lax.fori_loop(0, n // CONV_TB, block, 0)

    def norm_block(i, carry):
        start = pl.multiple_of(i * CONV_NORM_TB, CONV_NORM_TB)
        conv = conv_ref[pl.ds(start, CONV_NORM_TB), :]
        mu = jnp.mean(conv, axis=-1, keepdims=True)
        cen = conv - mu
        var = jnp.mean(cen * cen, axis=-1, keepdims=True)
        y = cen * lax.rsqrt(var + EPS) * lg_ref[...] + lb_ref[...]
        o_ref[0, pl.ds(start, CONV_NORM_TB), :] = (y * _sigmoid(y)).astype(o_ref.dtype)
        return carry

    lax.fori_loop(0, n // CONV_NORM_TB, norm_block, 0, unroll=4)


def _conformer(u, w, b, ln_g, ln_b):
    bsz, n, c = u.shape
    row = pl.BlockSpec((1, c), lambda i: (0, 0))
    return pl.pallas_call(
        functools.partial(_conformer_kernel, n=n),
        grid=(bsz,),
        in_specs=[pl.BlockSpec((1, n, c), lambda i: (i, 0, 0)),
                  pl.BlockSpec(w.shape, lambda i: (0, 0)), row, row, row],
        out_specs=pl.BlockSpec((1, n, c), lambda i: (i, 0, 0)),
        out_shape=jax.ShapeDtypeStruct((bsz, n, c), BF16),
        scratch_shapes=[pltpu.VMEM((n + 2 * CONV_PAD, c), F32), pltpu.VMEM((CONV_WIDTH, SUBLANES, c), F32),
                        pltpu.VMEM((n, c), F32)],
        compiler_params=_params("parallel"),
        name="conformer_conv",
    )(u, w, b.reshape(1, c), ln_g.reshape(1, c), ln_b.reshape(1, c))


def _head_of_lane():
    return lax.broadcasted_iota(jnp.int32, (1, HEAD_LANES), 1) // NA_HEAD_DIM


def _stack_heads(qr, head):
    zero = jnp.zeros_like(qr)
    return jnp.concatenate([jnp.where(head == h, qr, zero) for h in range(HEADS_PER_STEP)], axis=0)


def _unstack_heads(o, head, m):
    out = o[0:m]
    for h in range(1, HEADS_PER_STEP):
        out = jnp.where(head == h, o[h * m:(h + 1) * m], out)
    return out


def _build_na_bias(rpb_ref, bias_ref, group):
    qi = lax.broadcasted_iota(jnp.int32, (GRID_W, GRID_W), 0)
    ki = lax.broadcasted_iota(jnp.int32, (GRID_W, GRID_W), 1)
    rel = ki - qi
    col_start = jnp.clip(qi - NA_COLS // 2, 0, GRID_W - NA_COLS)
    valid = (ki >= col_start) & (ki < col_start + NA_COLS)
    n_r = 2 * NA_ROWS - 1
    n_c = 2 * NA_COLS - 1

    def per_head(h, carry):
        q0 = pl.multiple_of(h * GRID_W, GRID_W)
        for ri in range(n_r):
            base = ((group * HEADS_PER_STEP + h) * n_r + ri) * n_c
            t = jnp.full((GRID_W, GRID_W), NEG_INF, F32)
            for dc in range(n_c):
                t = jnp.where(rel == dc - (NA_COLS - 1), rpb_ref[base + dc] * LOG2_E, t)
            t = jnp.where(valid, t, NEG_INF)
            for d in range(NA_ROWS):
                j = ri + d - (NA_ROWS - 1)
                if 0 <= j < NA_ROWS:
                    bias_ref[d, pl.ds(q0, GRID_W), j * GRID_W:(j + 1) * GRID_W] = t
        return carry

    lax.fori_loop(0, HEADS_PER_STEP, per_head, 0)


def _lane_reduce(tiles, combine, reduce):
    by_width = {}
    for t in tiles:
        w = t.shape[-1]
        by_width[w] = t if w not in by_width else combine(by_width[w], t)
    parts = [reduce(t, axis=-1, keepdims=True) for t in by_width.values()]
    out = parts[0]
    for part in parts[1:]:
        out = combine(out, part)
    return out


def _na_latent_kernel(rpb_ref, q_ref, k_ref, v_ref, kc_ref, vc_ref, o_ref, bias_ref, *, rows, unroll):
    groups = q_ref.shape[-1] // HEAD_LANES

    @pl.when(pl.program_id(0) == 0)
    def _():
        for g in range(groups):
            _build_na_bias(rpb_ref, bias_ref.at[g], g)

    head = _head_of_lane()
    band = NA_ROWS * GRID_W
    nt = (((1,), (1,)), ((), ()))
    bounds = list(range(0, band, HEAD_LANES)) + [band]

    for g in range(groups):
        lanes = slice(g * HEAD_LANES, (g + 1) * HEAD_LANES)

        def row_step(r, carry, g=g, lanes=lanes):
            rs = jnp.clip(r - NA_ROWS // 2, 0, rows - NA_ROWS)
            q0 = pl.multiple_of(r * GRID_W, GRID_W)
            k0 = pl.multiple_of(rs * GRID_W, GRID_W)
            qs = _stack_heads(q_ref[0, pl.ds(q0, GRID_W), lanes], head)
            s_loc = lax.dot_general(qs, k_ref[0, pl.ds(k0, band), lanes], nt, preferred_element_type=F32)
            s_loc = s_loc + bias_ref[g, r - rs]
            s_ctx = lax.dot_general(qs, kc_ref[0, :, lanes], nt, preferred_element_type=F32)
            tiles = [s_loc[:, lo:hi] for lo, hi in zip(bounds[:-1], bounds[1:])] + [s_ctx]
            m = _lane_reduce(tiles, jnp.maximum, jnp.max)
            probs = [jnp.exp2(t - m) for t in tiles]
            inv = 1.0 / _lane_reduce(probs, jnp.add, jnp.sum)
            o = jnp.dot(jnp.concatenate(probs[:-1], axis=-1).astype(BF16), v_ref[0, pl.ds(k0, band), lanes],
                        preferred_element_type=F32)
            o = (o + jnp.dot(probs[-1].astype(BF16), vc_ref[0, :, lanes], preferred_element_type=F32)) * inv
            o_ref[0, pl.ds(q0, GRID_W), lanes] = _unstack_heads(o, head, GRID_W).astype(o_ref.dtype)
            return carry

        lax.fori_loop(0, rows, row_step, 0, unroll=unroll)


NA_VMEM_LIMIT_BYTES = 63 * 1024 * 1024
NA_ROW_UNROLL = 32


def _na_latent(q, k, v, kc, vc, rpb, *, unroll):
    bsz, n, na_dim = q.shape
    nc = kc.shape[1]
    groups = na_dim // HEAD_LANES
    rows = n // GRID_W
    unroll = min(unroll, rows)
    assert rows >= NA_ROWS and rows % unroll == 0
    lat = pl.BlockSpec((1, n, na_dim), lambda i: (i, 0, 0))
    ctx = pl.BlockSpec((1, nc, na_dim), lambda i: (i, 0, 0))
    return pl.pallas_call(
        functools.partial(_na_latent_kernel, rows=rows, unroll=unroll),
        grid=(bsz,),
        in_specs=[pl.BlockSpec(memory_space=pltpu.SMEM), lat, lat, lat, ctx, ctx],
        out_specs=lat,
        out_shape=jax.ShapeDtypeStruct((bsz, n, na_dim), BF16),
        scratch_shapes=[pltpu.VMEM((groups, NA_ROWS, HEADS_PER_STEP * GRID_W, NA_ROWS * GRID_W), F32)],
        compiler_params=pltpu.CompilerParams(dimension_semantics=("arbitrary",),
                                             vmem_limit_bytes=NA_VMEM_LIMIT_BYTES),
        name="na_latent",
    )(rpb.reshape(-1), q, k, v, kc, vc)


def _na_context_kernel(q_ref, k_ref, v_ref, o_ref):
    head = _head_of_lane()
    n = q_ref.shape[1]
    qs = _stack_heads(q_ref[0], head)
    s = lax.dot_general(qs, k_ref[0], (((1,), (1,)), ((), ())), preferred_element_type=F32)
    p = jnp.exp2(s - jnp.max(s, axis=-1, keepdims=True))
    denom = jnp.sum(p, axis=-1, keepdims=True)
    o = jnp.dot(p.astype(BF16), v_ref[0], preferred_element_type=F32) * (1.0 / denom)
    o_ref[0] = _unstack_heads(o, head, n).astype(o_ref.dtype)


def _na_context(q, k, v):
    bsz, n, na_dim = q.shape
    blk = pl.BlockSpec((1, n, HEAD_LANES), lambda i, g: (i, 0, g))
    return pl.pallas_call(
        _na_context_kernel,
        grid=(bsz, na_dim // HEAD_LANES),
        in_specs=[blk, blk, blk],
        out_specs=blk,
        out_shape=jax.ShapeDtypeStruct((bsz, n, na_dim), BF16),
        compiler_params=_params("parallel", "parallel"),
        name="na_context",
    )(q, k, v)


LRU_TB = 2048
LRU_SCAN_UNROLL = 16
LRU_PAD = 8


def _rglru_kernel(xl_ref, gl_ref, xc_ref, gc_ref, cw_ref, vec_ref, wx_ref, wa_ref, yl_ref, yc_ref,
                  padl_ref, padc_ref, a_ref, b_ref, h_ref, *, n_lat, n_ctx):
    c = xl_ref.shape[-1]
    zeros = jnp.zeros((LRU_PAD, c), F32)
    for pad_ref, src_ref, n in ((padc_ref, xc_ref, n_ctx), (padl_ref, xl_ref, n_lat)):
        pad_ref[0:LRU_PAD, :] = zeros
        pad_ref[n + LRU_PAD:n + 2 * LRU_PAD, :] = zeros
        pad_ref[LRU_PAD:n + LRU_PAD, :] = src_ref[0]
    sub = lax.broadcasted_iota(jnp.int32, (SUBLANES, c), 0)

    for direction in range(2):
        reverse = direction == 1
        conv_b = vec_ref[direction, 0:1, :]
        bx = vec_ref[direction, 1:2, :]
        ba = vec_ref[direction, 2:3, :]
        neg_lam = -vec_ref[direction, 3:4, :]
        softplus = jnp.maximum(neg_lam, 0.0) + jnp.log(1.0 + jnp.exp(-jnp.abs(neg_lam)))
        decay = -LRU_C * softplus

        conv_b8 = jnp.broadcast_to(conv_b, (SUBLANES, c))
        cw8 = [jnp.broadcast_to(cw_ref[direction, k:k + 1, :], (SUBLANES, c)) for k in range(LRU_CONV)]

        def coeffs(pad_ref, base, n):
            tb = min(n, LRU_TB)
            nt = tb // SUBLANES

            def block(i, carry):
                start = pl.multiple_of(i * tb, tb)
                tiles = [pad_ref[pl.ds(start + SUBLANES * m, SUBLANES), :]
                         for m in range(nt + 2 * LRU_PAD // SUBLANES)]
                u_tiles = [conv_b8] * nt
                for k in range(LRU_CONV):
                    off = LRU_PAD + k - (0 if reverse else LRU_CONV - 1)
                    sh = _shifted_tiles(tiles, off, nt, sub)
                    u_tiles = [u_tiles[j] + cw8[k] * sh[j] for j in range(nt)]
                u = jnp.concatenate(u_tiles, axis=0)
                ub = u.astype(BF16)
                gx = 1.0 / (1.0 + jnp.exp(jnp.dot(ub, wx_ref[direction], preferred_element_type=F32) - bx))
                ga = 1.0 / (1.0 + jnp.exp(jnp.dot(ub, wa_ref[direction], preferred_element_type=F32) - ba))
                a = jnp.exp(decay * ga)
                rem = 1.0 - a * a
                coef = jnp.where(rem > 0.0, rem * lax.rsqrt(rem), 0.0)
                a_ref[pl.ds(base + start, tb), :] = a
                b_ref[pl.ds(base + start, tb), :] = coef * (gx * u)
                return carry

            lax.fori_loop(0, n // tb, block, 0)

        coeffs(padc_ref, 0, n_ctx)
        coeffs(padl_ref, n_ctx, n_lat)

        def scan(base, n, hb):
            chunks = n // SUBLANES

            def chunk(j, hb):
                jj = chunks - 1 - j if reverse else j
                rows = pl.ds(pl.multiple_of(base + jj * SUBLANES, SUBLANES), SUBLANES)
                a = a_ref[rows, :]
                b = b_ref[rows, :]
                for d in (1, 2, 4):
                    shift = SUBLANES - d if reverse else d
                    keep = (sub < SUBLANES - d) if reverse else (sub >= d)
                    ra = pltpu.roll(a, shift, 0)
                    rb = pltpu.roll(b, shift, 0)
                    b = jnp.where(keep, a * rb + b, b)
                    a = jnp.where(keep, a * ra, a)
                h = a * hb + b
                if reverse:
                    h_ref[rows, :] = h_ref[rows, :] + h
                else:
                    h_ref[rows, :] = h
                last = 0 if reverse else SUBLANES - 1
                a_last = jnp.broadcast_to(a[last:last + 1, :], (SUBLANES, c))
                b_last = jnp.broadcast_to(b[last:last + 1, :], (SUBLANES, c))
                return a_last * hb + b_last

            return lax.fori_loop(0, chunks, chunk, hb, unroll=LRU_SCAN_UNROLL)

        hb = scan(0, n_ctx, jnp.zeros((SUBLANES, c), F32))
        scan(n_ctx, n_lat, hb)

    def gate_out(g_ref, y_ref, base, n):
        tb = min(n, LRU_TB)

        def block(i, carry):
            start = pl.multiple_of(i * tb, tb)
            g = g_ref[0, pl.ds(start, tb), :]
            y_ref[0, pl.ds(start, tb), :] = (_gelu_tanh(g) * h_ref[pl.ds(base + start, tb), :]).astype(y_ref.dtype)
            return carry

        lax.fori_loop(0, n // tb, block, 0)

    gate_out(gc_ref, yc_ref, 0, n_ctx)
    gate_out(gl_ref, yl_ref, n_ctx, n_lat)


def _block_diag(w):
    two, nb, m, _ = w.shape
    eye = jnp.eye(nb, dtype=w.dtype)
    return (w[:, :, :, None, :] * eye[None, :, None, :, None]).reshape(two, nb * m, nb * m)


def _rglru(xl, gl, xc, gc, conv_w, conv_b, wx, bx, wa, ba, lam):
    bsz, n_lat, c = xl.shape
    n_ctx = xc.shape[1]
    vec = jnp.stack([conv_b, bx, ba, lam], axis=1)
    wxd = _block_diag(-wx).astype(BF16)
    wad = _block_diag(-wa).astype(BF16)
    lat = pl.BlockSpec((1, n_lat, c), lambda i: (i, 0, 0))
    ctx = pl.BlockSpec((1, n_ctx, c), lambda i: (i, 0, 0))
    full = lambda a: pl.BlockSpec(a.shape, lambda i: (0,) * a.ndim)
    return pl.pallas_call(
        functools.partial(_rglru_kernel, n_lat=n_lat, n_ctx=n_ctx),
        grid=(bsz,),
        in_specs=[lat, lat, ctx, ctx, full(conv_w), full(vec), full(wxd), full(wad)],
        out_specs=[lat, ctx],
        out_shape=[jax.ShapeDtypeStruct((bsz, n_lat, c), BF16), jax.ShapeDtypeStruct((bsz, n_ctx, c), BF16)],
        scratch_shapes=[pltpu.VMEM((n_lat + 2 * LRU_PAD, c), F32), pltpu.VMEM((n_ctx + 2 * LRU_PAD, c), F32),
                        pltpu.VMEM((n_ctx + n_lat, c), F32), pltpu.VMEM((n_ctx + n_lat, c), F32),
                        pltpu.VMEM((n_ctx + n_lat, c), F32)],
        compiler_params=_params("parallel"),
        name="rglru",
    )(xl, gl, xc, gc, conv_w, vec, wxd, wad)


FF_CHUNK = 1024


def _out_mlp_body(x, ya, yb, yc, mod_ref, g2_ref, gf_ref, wo_ref, w1_ref, w2_ref, final_norm):
    ca, cb, cc = ya.shape[-1], yb.shape[-1], yc.shape[-1]
    mix = (jnp.dot(ya, wo_ref[0:ca, :], preferred_element_type=F32)
           + jnp.dot(yb, wo_ref[ca:ca + cb, :], preferred_element_type=F32)
           + jnp.dot(yc, wo_ref[ca + cb:ca + cb + cc, :], preferred_element_type=F32))
    x1 = x + mod_ref[0, 2:3, :] * mix
    h2 = _rms(x1) * g2_ref[...]
    h2 = (h2 * (1.0 + mod_ref[0, 4:5, :]) + mod_ref[0, 3:4, :]).astype(BF16)
    acc = jnp.zeros(x1.shape, F32)
    for j in range(w1_ref.shape[1] // FF_CHUNK):
        cols = slice(j * FF_CHUNK, (j + 1) * FF_CHUNK)
        hid = jnp.maximum(jnp.dot(h2, w1_ref[:, cols], preferred_element_type=F32), 0.0)
        acc = acc + jnp.dot((hid * hid).astype(BF16), w2_ref[cols, :], preferred_element_type=F32)
    out = x1 + mod_ref[0, 5:6, :] * acc
    if final_norm:
        out = _rms(out) * gf_ref[...]
    return out


def _out_mlp_kernel(x_ref, ya_ref, yb_ref, yc_ref, mod_ref, g2_ref, gf_ref, wo_ref, w1_ref, w2_ref, o_ref, *,
                    final_norm):
    o_ref[0] = _out_mlp_body(x_ref[0], ya_ref[0], yb_ref[0], yc_ref[0], mod_ref, g2_ref, gf_ref,
                             wo_ref, w1_ref, w2_ref, final_norm)


def _conformer_tile(u_ref, w8_ref, bias, lg_ref, lb_ref, ya_ref, slot, tile, *, n, tm, sub):
    half = CONV_WIDTH // 2
    out_tiles = CONV_TB // SUBLANES
    win_tiles = (CONV_TB + 2 * CONV_PAD) // SUBLANES
    pad_tiles = CONV_PAD // SUBLANES
    blocks = tm // CONV_TB
    first = tile == 0
    last = tile == n // tm - 1
    zero = jnp.zeros((SUBLANES, u_ref.shape[-1]), F32)
    done = []
    for blk in range(blocks):
        base = tile * tm + (CONV_TB * blk - CONV_PAD)
        tiles = []
        for m in range(win_tiles):
            start = base + SUBLANES * m
            if blk == 0 and m < pad_tiles:
                t = u_ref[0, pl.ds(pl.multiple_of(jnp.maximum(start, 0), SUBLANES), SUBLANES), :]
                t = jnp.where(first, zero, t)
            elif blk == blocks - 1 and m >= win_tiles - pad_tiles:
                t = u_ref[0, pl.ds(pl.multiple_of(jnp.minimum(start, n - SUBLANES), SUBLANES), SUBLANES), :]
                t = jnp.where(last, zero, t)
            else:
                t = u_ref[0, pl.ds(pl.multiple_of(start, SUBLANES), SUBLANES), :]
            tiles.append(t)
        acc = [bias] * out_tiles
        for r in range(SUBLANES):
            sh = _shifted_tiles(tiles, r, win_tiles - 1, sub)
            for a in range(win_tiles - out_tiles):
                k = r + SUBLANES * a - (CONV_PAD - half)
                if 0 <= k < CONV_WIDTH:
                    w = w8_ref[k]
                    acc = [acc[j] + w * sh[j + a] for j in range(out_tiles)]
        done.extend(acc)
        if len(done) == CONV_NORM_TB // SUBLANES:
            conv = jnp.concatenate(done, axis=0)
            done = []
            mu = jnp.mean(conv, axis=-1, keepdims=True)
            cen = conv - mu
            var = jnp.mean(cen * cen, axis=-1, keepdims=True)
            y = cen * lax.rsqrt(var + EPS) * lg_ref[...] + lb_ref[...]
            row0 = CONV_TB * (blk + 1) - CONV_NORM_TB
            ya_ref[slot, row0:row0 + CONV_NORM_TB, :] = (y * _sigmoid(y)).astype(ya_ref.dtype)


def _conv_out_mlp_kernel(x_ref, u_ref, yb_ref, yc_ref, cw_ref, cb_ref, lg_ref, lb_ref, mod_ref, g2_ref, gf_ref,
                         wo_ref, w1_ref, w2_ref, o_ref, ya_ref, w8_ref, *, final_norm, n, tm):
    c = u_ref.shape[-1]
    tiles_per_seq = n // tm
    step = pl.program_id(0) * tiles_per_seq + pl.program_id(1)
    cur = step % 2
    for k in range(CONV_WIDTH):
        w8_ref[k] = jnp.broadcast_to(cw_ref[k:k + 1, :], (SUBLANES, c))
    sub = lax.broadcasted_iota(jnp.int32, (SUBLANES, c), 0)
    bias = jnp.broadcast_to(cb_ref[...], (SUBLANES, c))
    conv = functools.partial(_conformer_tile, u_ref, w8_ref, bias, lg_ref, lb_ref, ya_ref, n=n, tm=tm, sub=sub)

    @pl.when(step == 0)
    def _():
        conv(0, 0)

    o_ref[0] = _out_mlp_body(x_ref[0], ya_ref[cur], yb_ref[0], yc_ref[0], mod_ref, g2_ref, gf_ref,
                             wo_ref, w1_ref, w2_ref, final_norm)
    conv(1 - cur, (pl.program_id(1) + 1) % tiles_per_seq)


def _out_mlp(x, ya, yb, yc, mod, g2, gf, wo, w1, w2, layer, *, final_norm):
    b, n, d = x.shape
    tm = min(n, 512)
    per_batch = mod.shape[0] != 1
    tok = lambda a: pl.BlockSpec((1, tm, a.shape[-1]), lambda i, j: (i, j, 0))
    resident = lambda a: _layer_weight(a, layer, pipeline_mode=pl.Buffered(1))
    return pl.pallas_call(
        functools.partial(_out_mlp_kernel, final_norm=final_norm),
        grid=(b, n // tm),
        in_specs=[tok(x), tok(ya), tok(yb), tok(yc),
                  pl.BlockSpec((1, 6, d), (lambda i, j: (i, 0, 0)) if per_batch else (lambda i, j: (0, 0, 0))),
                  pl.BlockSpec((1, d), lambda i, j: (0, 0)), pl.BlockSpec((1, d), lambda i, j: (0, 0)),
                  resident(wo), resident(w1), resident(w2)],
        out_specs=tok(x),
        out_shape=jax.ShapeDtypeStruct((b, n, d), F32),
        compiler_params=_params("parallel", "parallel"),
        name="out_proj_mlp",
    )(x, ya, yb, yc, mod, g2, gf, wo, w1, w2)


def _conv_out_mlp(x, u, yb, yc, conv_w, conv_b, ln_g, ln_b, mod, g2, gf, wo, w1, w2, layer, *, final_norm):
    b, n, d = x.shape
    c = u.shape[-1]
    tm = min(n, 512)
    nt = n // tm
    assert tm % CONV_NORM_TB == 0 and CONV_NORM_TB % CONV_TB == 0
    tok = lambda a: pl.BlockSpec((1, tm, a.shape[-1]), lambda i, j: (i, j, 0))
    row = lambda width: pl.BlockSpec((1, width), lambda i, j: (0, 0))
    resident = lambda a: _layer_weight(a, layer, pipeline_mode=pl.Buffered(1))
    next_batch = lambda i, j: (jnp.minimum(i + (j + 1) // nt, b - 1), 0, 0)
    return pl.pallas_call(
        functools.partial(_conv_out_mlp_kernel, final_norm=final_norm, n=n, tm=tm),
        grid=(b, nt),
        in_specs=[tok(x), pl.BlockSpec((1, n, c), next_batch), tok(yb), tok(yc),
                  pl.BlockSpec(conv_w.shape, lambda i, j: (0, 0)), row(c), row(c), row(c),
                  pl.BlockSpec((1, 6, d), lambda i, j: (i, 0, 0)), row(d), row(d),
                  resident(wo), resident(w1), resident(w2)],
        out_specs=tok(x),
        out_shape=jax.ShapeDtypeStruct((b, n, d), F32),
        scratch_shapes=[pltpu.VMEM((2, tm, c), BF16), pltpu.VMEM((CONV_WIDTH, SUBLANES, c), F32)],
        compiler_params=_params("arbitrary", "arbitrary"),
        name="conv_out_proj_mlp",
    )(x, u, yb, yc, conv_w, conv_b.reshape(1, c), ln_g.reshape(1, c), ln_b.reshape(1, c), mod, g2, gf, wo, w1, w2)


def kernel(x, c, ctx, c_ctx, norm1_g, norm2_g, ada_w, ada_b, w_in, w_out, conv_w, conv_b, conv_ln_g, conv_ln_b,
           na_rpb, lru_conv_w, lru_conv_b, lru_wx, lru_bx, lru_wa, lru_ba, lru_lambda, mlp_w1, mlp_w2, final_g):
    depth = w_in.shape[0]
    bsz, _, d = x.shape
    conv_dim = conv_w.shape[-1]
    lru_dim = lru_conv_w.shape[-1]
    na_dim = NA_HEADS * NA_HEAD_DIM
    dims = dict(conv_dim=conv_dim, na_dim=na_dim, lru_dim=lru_dim)

    cond = jnp.concatenate([c, c_ctx[None, :]], axis=0)
    cond = jnp.pad(cond, ((0, -(bsz + 1) % SUBLANES), (0, 0)))
    mod_all = _modulation(cond, ada_w, ada_b)
    w_in_b, w_out_b = w_in.astype(BF16), w_out.astype(BF16)
    w1_b, w2_b = mlp_w1.astype(BF16), mlp_w2.astype(BF16)
    cx = ctx
    for l in range(depth):
        last = l == depth - 1
        mod = mod_all[l, :bsz].reshape(bsz, 6, d)
        mod_c = mod_all[l, bsz:bsz + 1].reshape(1, 6, d)
        g1 = norm1_g[l].reshape(1, d)

        lat = _in_proj(x, mod, g1, w_in_b, l, **dims)
        cxp = _in_proj(cx, mod_c, g1, w_in_b, l, **dims, want=("k", "v", "rx", "rg") if last else IN_PROJ_OUTPUTS)

        y_b = _na_latent(lat["q"], lat["k"], lat["v"], cxp["k"], cxp["v"], na_rpb[l], unroll=NA_ROW_UNROLL)
        y_c, yc_c = _rglru(lat["rx"], lat["rg"], cxp["rx"], cxp["rg"], lru_conv_w[l], lru_conv_b[l],
                           lru_wx[l], lru_bx[l], lru_wa[l], lru_ba[l], lru_lambda[l])

        g2 = norm2_g[l].reshape(1, d)
        gf = final_g.reshape(1, d)
        x = _conv_out_mlp(x, lat["u"], y_b, y_c, conv_w[l], conv_b[l], conv_ln_g[l], conv_ln_b[l], mod, g2, gf,
                          w_out_b, w1_b, w2_b, l, final_norm=last)
        if not last:
            yc_a = _conformer(cxp["u"], conv_w[l], conv_b[l], conv_ln_g[l], conv_ln_b[l])
            yc_b = _na_context(cxp["q"], cxp["k"], cxp["v"])
            cx = _out_mlp(cx, yc_a, yc_b, yc_c, mod_c, g2, gf, w_out_b, w1_b, w2_b, l, final_norm=False)
    return x
```
